```python
import math
import jax
import jax.numpy as jnp
from jax import lax
import numpy as np

D_MODEL = 2048
BATCH = 2
SEQ = 8192
DEPTH = 2

HEAD_DIM = 128
D_MIX = D_MODEL
N_MIX_HEADS = D_MIX // HEAD_DIM
RET_HEADS = N_MIX_HEADS // 4
GDN_HEADS = (N_MIX_HEADS - RET_HEADS) // 2
MOBA_HEADS = N_MIX_HEADS - RET_HEADS - GDN_HEADS
RET_W = RET_HEADS * HEAD_DIM
GDN_W = GDN_HEADS * HEAD_DIM
MOBA_W = MOBA_HEADS * HEAD_DIM
IN_WIDTH = 4 * RET_W + 4 * GDN_W + 2 * GDN_HEADS + 4 * MOBA_W
RET_CHUNK = 128
GDN_CHUNK = 64
GDN_CONV = 4
MOBA_BLOCK = 256
MOBA_TOP_K = 3
MOBA_Q_BLOCK = 64
RMS_EPS = 1e-6

kernel_name = 'hymba_style_retention_gdn_moba_block'


def rms_norm(x, w):
    x32 = x.astype(jnp.float32)
    y = x32 * lax.rsqrt(jnp.mean(x32 * x32, axis=-1, keepdims=True) + RMS_EPS)
    return (y * w).astype(x.dtype)


def l2_norm(x):
    x32 = x.astype(jnp.float32)
    return (x32 * lax.rsqrt(jnp.sum(x32 * x32, axis=-1, keepdims=True) + RMS_EPS)).astype(x.dtype)


def to_heads(t, n_heads):
    b, s, _ = t.shape
    return t.reshape(b, s, n_heads, -1).transpose(0, 2, 1, 3)


def from_heads(t):
    b, h, s, d = t.shape
    return t.transpose(0, 2, 1, 3).reshape(b, s, h * d)


def causal_depthwise_conv(x, w):
    k, c = w.shape
    return lax.conv_general_dilated(x, w[:, None, :].astype(x.dtype), window_strides=(1,),
                                    padding=[(k - 1, 0)], dimension_numbers=('NWC', 'WIO', 'NWC'),
                                    feature_group_count=c)


def retention(q, k, v):
    in_dtype = v.dtype
    q, k, v = (t.astype(jnp.float32) for t in (q, k, v))
    b, h, s, d = q.shape
    c = RET_CHUNK
    n = s // c
    q = q.reshape(b, h, n, c, d)
    k = k.reshape(b, h, n, c, d)
    v = v.reshape(b, h, n, c, d)
    log_gamma = jnp.log1p(-jnp.exp2(-5.0 - jnp.arange(h, dtype=jnp.float32)))[:, None]
    pos = jnp.arange(c, dtype=jnp.float32)
    diff = pos[:, None] - pos[None, :]
    decay_mat = jnp.where(diff >= 0, jnp.exp(jnp.maximum(diff, 0.0) * log_gamma[:, :, None]), 0.0)
    scores = jnp.einsum('bhncd,bhnsd->bhncs', q, k) * decay_mat[None, :, None]
    intra = jnp.einsum('bhncs,bhnse->bhnce', scores, v)
    k_w = k * jnp.exp((c - 1 - pos) * log_gamma)[None, :, None, :, None]
    kv = jnp.einsum('bhncd,bhnce->nbhde', k_w, v)
    chunk_decay = jnp.exp(c * log_gamma)[:, :, None]

    def step(state, kv_n):
        return state * chunk_decay + kv_n, state

    _, s_prev = lax.scan(step, jnp.zeros((b, h, d, d), jnp.float32), kv)
    q_w = q * jnp.exp((pos + 1.0) * log_gamma)[None, :, None, :, None]
    cross = jnp.einsum('bhncd,nbhde->bhnce', q_w, s_prev)
    return (intra + cross).reshape(b, h, s, d).astype(in_dtype)


def gated_delta_rule(q, k, v, g, beta):
    in_dtype = v.dtype
    q, k, v, g, beta = (t.astype(jnp.float32) for t in (q, k, v, g, beta))
    b, h, s, d = q.shape
    c = GDN_CHUNK
    n = s // c
    q = q.reshape(b, h, n, c, d)
    k = k.reshape(b, h, n, c, d)
    v = v.reshape(b, h, n, c, d)
    g_cum = jnp.cumsum(g.reshape(b, h, n, c), axis=-1)
    beta = beta.reshape(b, h, n, c)
    causal = jnp.tril(jnp.ones((c, c), dtype=bool))
    strict = jnp.tril(jnp.ones((c, c), dtype=bool), -1)
    decay = jnp.exp(jnp.where(causal, g_cum[..., :, None] - g_cum[..., None, :], -jnp.inf))
    k_beta = k * beta[..., None]
    m = jnp.where(strict, jnp.einsum('bhncd,bhnsd->bhncs', k_beta, k) * decay, 0.0)
    rhs = jnp.concatenate([k_beta * jnp.exp(g_cum)[..., None], v * beta[..., None]], axis=-1)
    wu = lax.linalg.triangular_solve(m + jnp.eye(c, dtype=jnp.float32), rhs, left_side=True,
                                     lower=True, unit_diagonal=True)
    w, u = wu[..., :d], wu[..., d:]
    qk = jnp.einsum('bhncd,bhnsd->bhncs', q, k) * decay
    q_g = q * jnp.exp(g_cum)[..., None]
    g_last = g_cum[..., -1]
    k_g = k * jnp.exp(g_last[..., None] - g_cum)[..., None]

    def step(state, inp):
        q_n, k_n, w_n, u_n, qk_n, gl_n = inp
        v_new = u_n - jnp.einsum('bhcd,bhde->bhce', w_n, state)
        o = jnp.einsum('bhcd,bhde->bhce', q_n, state) + jnp.einsum('bhcs,bhse->bhce', qk_n, v_new)
        state = state * jnp.exp(gl_n)[..., None, None] + jnp.einsum('bhcd,bhce->bhde', k_n, v_new)
        return state, o

    xs = tuple(jnp.moveaxis(t, 2, 0) for t in (q_g, k_g, w, u, qk, g_last))
    _, o = lax.scan(step, jnp.zeros((b, h, d, d), jnp.float32), xs)
    return jnp.moveaxis(o, 0, 2).reshape(b, h, s, d).astype(in_dtype)


def moba_attention(q, k, v):
    b, h, s, d = q.shape
    n_blocks = -(-s // MOBA_BLOCK)
    s_pad = n_blocks * MOBA_BLOCK
    pad = ((0, 0), (0, 0), (0, s_pad - s), (0, 0))
    q, k, v = jnp.pad(q, pad), jnp.pad(k, pad), jnp.pad(v, pad)
    k_blocks = k.reshape(b, h, n_blocks, MOBA_BLOCK, d)
    v_blocks = v.reshape(b, h, n_blocks, MOBA_BLOCK, d)
    k_mean = jnp.mean(k_blocks.astype(jnp.float32), axis=3)
    k_sel = min(MOBA_TOP_K, n_blocks)
    scale = d ** -0.5
    slopes = jnp.exp2(-8.0 * jnp.arange(1, h + 1, dtype=jnp.float32) / h)
    bi = jnp.arange(b)[:, None, None, None]
    hi = jnp.arange(h)[None, :, None, None]
    blk_off = jnp.arange(MOBA_BLOCK)

    def one_query_block(i):
        start = i * MOBA_Q_BLOCK
        own = start // MOBA_BLOCK
        qc = lax.dynamic_slice_in_dim(q, start, MOBA_Q_BLOCK, axis=2)
        qpos = start + jnp.arange(MOBA_Q_BLOCK)
        gate = jnp.einsum('bhqd,bhnd->bhqn', qc.astype(jnp.float32), k_mean)
        gate = jnp.where(jnp.arange(n_blocks) < own, gate, -jnp.inf)
        gval, gidx = lax.top_k(gate, k_sel)
        sel_ok = jnp.isfinite(gval)
        k_g = k_blocks[bi, hi, gidx]
        kpos = gidx[..., None] * MOBA_BLOCK + blk_off
        dist_sel = (qpos[None, None, :, None, None] - kpos).astype(jnp.float32)
        s_sel = (jnp.einsum('bhqd,bhqjsd->bhqjs', qc, k_g).astype(jnp.float32) * scale
                 - slopes[None, :, None, None, None] * dist_sel)
        s_sel = jnp.where(sel_ok[..., None], s_sel, -jnp.inf)
        own_k = lax.dynamic_slice_in_dim(k, own * MOBA_BLOCK, MOBA_BLOCK, axis=2)
        own_v = lax.dynamic_slice_in_dim(v, own * MOBA_BLOCK, MOBA_BLOCK, axis=2)
        own_pos = own * MOBA_BLOCK + blk_off
        dist_own = (qpos[:, None] - own_pos[None, :]).astype(jnp.float32)
        s_own = (jnp.einsum('bhqd,bhsd->bhqs', qc, own_k).astype(jnp.float32) * scale
                 - slopes[:, None, None] * dist_own)
        s_own = jnp.where(own_pos[None, :] <= qpos[:, None], s_own, -jnp.inf)
        logits = jnp.concatenate([s_sel.reshape(b, h, MOBA_Q_BLOCK, k_sel * MOBA_BLOCK), s_own], axis=-1)
        p = jax.nn.softmax(logits, axis=-1)
        p_sel = p[..., :k_sel * MOBA_BLOCK].reshape(b, h, MOBA_Q_BLOCK, k_sel, MOBA_BLOCK)
        p_own = p[..., k_sel * MOBA_BLOCK:]
        v_g = v_blocks[bi, hi, gidx]
        o = jnp.einsum('bhqjs,bhqjsd->bhqd', p_sel, v_g) + jnp.einsum('bhqs,bhsd->bhqd', p_own, own_v)
        return o.astype(q.dtype)

    out = lax.map(one_query_block, jnp.arange(s_pad // MOBA_Q_BLOCK))
    out = jnp.moveaxis(out, 0, 2).reshape(b, h, s_pad, d)
    return out[:, :, :s]


def hybrid_layer(x, norm_w, w_in, conv_w, a_log, dt_bias, ret_norm_w, gdn_norm_w, q_norm_w, k_norm_w, w_out):
    h = rms_norm(x, norm_w)
    proj = jnp.einsum('btd,de->bte', h, w_in)
    sizes = (RET_W,) * 4 + (3 * GDN_W, GDN_W, GDN_HEADS, GDN_HEADS) + (MOBA_W,) * 4
    offsets = [int(o) for o in np.cumsum(sizes)[:-1]]
    rq, rk, rv, rg, gqkv, gz, ga, gb, mq, mk, mv, mz = jnp.split(proj, offsets, axis=-1)

    o_ret = retention(to_heads(rq, RET_HEADS), to_heads(rk, RET_HEADS) * HEAD_DIM ** -0.5,
                      to_heads(rv, RET_HEADS))
    y_ret = from_heads(rms_norm(o_ret, ret_norm_w[:, None, :])) * jax.nn.silu(rg)

    qkv = jax.nn.silu(causal_depthwise_conv(gqkv, conv_w))
    gq, gk, gv = jnp.split(qkv, 3, axis=-1)
    gq = l2_norm(to_heads(gq, GDN_HEADS)) * HEAD_DIM ** -0.5
    gk = l2_norm(to_heads(gk, GDN_HEADS))
    gv = to_heads(gv, GDN_HEADS)
    beta = jax.nn.sigmoid(gb.astype(jnp.float32)).transpose(0, 2, 1)
    g = (-jnp.exp(a_log.astype(jnp.float32))
         * jax.nn.softplus(ga.astype(jnp.float32) + dt_bias.astype(jnp.float32))).transpose(0, 2, 1)
    o_gdn = gated_delta_rule(gq, gk, gv, g, beta)
    y_gdn = from_heads(rms_norm(o_gdn, gdn_norm_w)) * jax.nn.silu(gz)

    mq = rms_norm(to_heads(mq, MOBA_HEADS), q_norm_w)
    mk = rms_norm(to_heads(mk, MOBA_HEADS), k_norm_w)
    o_moba = moba_attention(mq, mk, to_heads(mv, MOBA_HEADS))
    y_moba = from_heads(o_moba) * jax.nn.silu(mz)

    y = jnp.concatenate([y_ret, y_gdn, y_moba], axis=-1)
    return x + jnp.einsum('bte,ed->btd', y, w_out).astype(x.dtype)


def setup_inputs(seed: int = 0) -> dict:
    key = jax.random.key(seed)
    ks = jax.random.split(key, 12)
    f32 = jnp.float32
    x = jax.random.normal(ks[0], (BATCH, SEQ, D_MODEL), f32)
    norm_w = 1.0 + 0.02 * jax.random.normal(ks[1], (DEPTH, D_MODEL), f32)
    w_in = jax.random.normal(ks[2], (DEPTH, D_MODEL, IN_WIDTH), f32) * D_MODEL ** -0.5
    conv_w = jax.random.normal(ks[3], (DEPTH, GDN_CONV, 3 * GDN_W), f32) * GDN_CONV ** -0.5
    a_log = jnp.log(jax.random.uniform(ks[4], (DEPTH, GDN_HEADS), f32, minval=1.0, maxval=16.0))
    dt = jnp.exp(jax.random.uniform(ks[5], (DEPTH, GDN_HEADS), f32,
                                    minval=math.log(1e-3), maxval=math.log(0.1)))
    dt_bias = dt + jnp.log(-jnp.expm1(-dt))
    ret_norm_w = 1.0 + 0.02 * jax.random.normal(ks[6], (DEPTH, RET_HEADS, HEAD_DIM), f32)
    gdn_norm_w = 1.0 + 0.02 * jax.random.normal(ks[7], (DEPTH, HEAD_DIM), f32)
    q_norm_w = 1.0 + 0.02 * jax.random.normal(ks[8], (DEPTH, HEAD_DIM), f32)
    k_norm_w = 1.0 + 0.02 * jax.random.normal(ks[9], (DEPTH, HEAD_DIM), f32)
    w_out = jax.random.normal(ks[10], (DEPTH, D_MIX, D_MODEL), f32) * D_MIX ** -0.5
    return {'x': x, 'norm_w': norm_w, 'w_in': w_in, 'conv_w': conv_w, 'a_log': a_log,
            'dt_bias': dt_bias, 'ret_norm_w': ret_norm_w, 'gdn_norm_w': gdn_norm_w,
            'q_norm_w': q_norm_w, 'k_norm_w': k_norm_w, 'w_out': w_out}


def reference(x, norm_w, w_in, conv_w, a_log, dt_bias, ret_norm_w, gdn_norm_w, q_norm_w, k_norm_w, w_out):
    for layer in range(DEPTH):
        x = hybrid_layer(x, norm_w[layer], w_in[layer], conv_w[layer], a_log[layer], dt_bias[layer],
                         ret_norm_w[layer], gdn_norm_w[layer], q_norm_w[layer], k_norm_w[layer],
                         w_out[layer])
    return x
```

```python
import functools

import numpy as np
import jax
import jax.numpy as jnp
from jax import lax
from jax.experimental import pallas as pl
from jax.experimental.pallas import tpu as pltpu

F32 = jnp.float32
BF16 = jnp.bfloat16

HEAD_DIM = 128
RET_HEADS = 4
GDN_HEADS = 6
MOBA_HEADS = 6
RET_W = RET_HEADS * HEAD_DIM
GDN_W = GDN_HEADS * HEAD_DIM
MOBA_W = MOBA_HEADS * HEAD_DIM
GDN_CONV = 4
MOBA_BLOCK = 256
MOBA_TOP_K = 3
RMS_EPS = 1e-6
SCALE = HEAD_DIM ** -0.5

GDN_OFF = 0
MOBA_OFF = 4 * GDN_W
RET_OFF = 4 * GDN_W + 4 * MOBA_W
MAIN_W = RET_OFF + 4 * RET_W
SMALL_W = 128

RET_C = 256
GDN_C = 128
HALO = 8

NEG_BIG = -1e30
SEL_LANES = 32
VMEM_LIMIT = 56 * 1024 * 1024


def _dot(a, b):
    return jnp.dot(a, b, preferred_element_type=F32)


def _dot_nt(a, b):
    return lax.dot_general(a, b, (((1,), (1,)), ((), ())), preferred_element_type=F32)


def _dot_tn(a, b):
    return lax.dot_general(a, b, (((0,), (0,)), ((), ())), preferred_element_type=F32)


def _sigmoid(x):
    return 1.0 / (1.0 + jnp.exp(-x))


def _silu(x):
    return x * _sigmoid(x)


def _split3(x):
    x1 = x.astype(BF16)
    r = x - x1.astype(F32)
    x2 = r.astype(BF16)
    x3 = (r - x2.astype(F32)).astype(BF16)
    return x1, x2, x3


def _in_proj_kernel(x_ref, nw_ref, w_ref, ws_ref, o_ref, os_ref, h_ref):
    @pl.when(pl.program_id(1) == 0)
    def _():
        x = x_ref[...]
        ms = jnp.mean(x * x, axis=-1, keepdims=True)
        h = (x * lax.rsqrt(ms + RMS_EPS) * nw_ref[...]).astype(BF16)
        h_ref[...] = h
        os_ref[...] = _dot(h, ws_ref[...])

    o_ref[...] = _dot(h_ref[...], w_ref[...]).astype(o_ref.dtype)


def _in_proj(x2d, norm_w, w_main, w_small, tm=1024, tn=512):
    m, d = x2d.shape
    n = w_main.shape[1]
    return pl.pallas_call(
        _in_proj_kernel,
        grid=(m // tm, n // tn),
        in_specs=[
            pl.BlockSpec((tm, d), lambda i, j: (i, 0)),
            pl.BlockSpec((1, d), lambda i, j: (0, 0)),
            pl.BlockSpec((d, tn), lambda i, j: (0, j)),
            pl.BlockSpec((d, SMALL_W), lambda i, j: (0, 0)),
        ],
        out_specs=[
            pl.BlockSpec((tm, tn), lambda i, j: (i, j)),
            pl.BlockSpec((tm, SMALL_W), lambda i, j: (i, 0)),
        ],
        out_shape=[
            jax.ShapeDtypeStruct((m, n), BF16),
            jax.ShapeDtypeStruct((m, SMALL_W), F32),
        ],
        scratch_shapes=[pltpu.VMEM((tm, d), BF16)],
        compiler_params=pltpu.CompilerParams(
            dimension_semantics=("parallel", "arbitrary"), vmem_limit_bytes=VMEM_LIMIT),
        name="in_proj",
    )(x2d, norm_w.reshape(1, d), w_main, w_small)


def _out_proj_kernel(x_ref, yr_ref, yg_ref, ym_ref, w_ref, o_ref, *, tn):
    n = o_ref.shape[1]
    yr = yr_ref[...]
    yg = yg_ref[...]
    ym = ym_ref[...]
    for c in range(0, n, tn):
        acc = x_ref[:, c:c + tn]
        acc = acc + _dot(yr, w_ref[0:RET_W, c:c + tn])
        acc = acc + _dot(yg, w_ref[RET_W:RET_W + GDN_W, c:c + tn])
        acc = acc + _dot(ym, w_ref[RET_W + GDN_W:RET_W + GDN_W + MOBA_W, c:c + tn])
        o_ref[:, c:c + tn] = acc


def _out_proj(x2d, y_ret, y_gdn, y_moba, w_out, tm=512, tn=512):
    m, d = x2d.shape
    return pl.pallas_call(
        functools.partial(_out_proj_kernel, tn=tn),
        grid=(m // tm,),
        in_specs=[
            pl.BlockSpec((tm, d), lambda i: (i, 0)),
            pl.BlockSpec((tm, RET_W), lambda i: (i, 0)),
            pl.BlockSpec((tm, GDN_W), lambda i: (i, 0)),
            pl.BlockSpec((tm, MOBA_W), lambda i: (i, 0)),
            pl.BlockSpec(w_out.shape, lambda i: (0, 0)),
        ],
        out_specs=pl.BlockSpec((tm, d), lambda i: (i, 0)),
        out_shape=jax.ShapeDtypeStruct((m, d), F32),
        compiler_params=pltpu.CompilerParams(
            dimension_semantics=("parallel",), vmem_limit_bytes=VMEM_LIMIT),
        name="out_proj",
    )(x2d, y_ret, y_gdn, y_moba, w_out)


def _retention_consts(c):
    h = np.arange(RET_HEADS, dtype=np.float64)
    log_gamma = np.log1p(-np.exp2(-5.0 - h))
    pos = np.arange(c, dtype=np.float64)
    diff = pos[:, None] - pos[None, :]
    dmat = np.where(diff >= 0, np.exp(np.maximum(diff, 0.0) * log_gamma[:, None, None]), 0.0) * SCALE
    kws = np.exp((c - 1 - pos)[None, :] * log_gamma[:, None])
    qws = np.exp((pos + 1.0)[None, :] * log_gamma[:, None]) * SCALE
    kws = np.broadcast_to(kws[:, :, None], (RET_HEADS, c, HEAD_DIM))
    qws = np.broadcast_to(qws[:, :, None], (RET_HEADS, c, HEAD_DIM))
    chunk_decay = [float(v) for v in np.exp(c * log_gamma)]
    return (jnp.asarray(dmat, F32), jnp.asarray(kws, F32), jnp.asarray(qws, F32), chunk_decay)


def _retention_kernel(q_ref, k_ref, v_ref, g_ref, dm_ref, kws_ref, qws_ref, nw_ref, o_ref, st_ref,
                      *, chunk_decay):
    @pl.when(pl.program_id(1) == 0)
    def _():
        st_ref[...] = jnp.zeros_like(st_ref)

    for h in range(RET_HEADS):
        sl = slice(h * HEAD_DIM, (h + 1) * HEAD_DIM)
        q = q_ref[:, sl]
        k = k_ref[:, sl]
        v = v_ref[:, sl]
        s = _dot_nt(q, k) * dm_ref[h]
        intra = _dot(s.astype(BF16), v)
        state = st_ref[h]
        qw = (q.astype(F32) * qws_ref[h]).astype(BF16)
        o = intra + _dot(qw, state.astype(BF16))
        kw = (k.astype(F32) * kws_ref[h]).astype(BF16)
        st_ref[h] = state * chunk_decay[h] + _dot_tn(kw, v)
        ms = jnp.mean(o * o, axis=-1, keepdims=True)
        y = o * lax.rsqrt(ms + RMS_EPS) * nw_ref[h:h + 1, :]
        o_ref[:, sl] = (y * _silu(g_ref[:, sl].astype(F32))).astype(o_ref.dtype)


def _retention(proj, ret_norm_w, b, t):
    c = RET_C
    nt = t // c
    dmat, kws, qws, chunk_decay = _retention_consts(c)
    base = RET_OFF // RET_W

    def col(ci):
        return pl.BlockSpec((c, RET_W), lambda bi, ti: (bi * nt + ti, base + ci))

    def const3(shape):
        return pl.BlockSpec(shape, lambda bi, ti: (0, 0, 0))

    return pl.pallas_call(
        functools.partial(_retention_kernel, chunk_decay=chunk_decay),
        grid=(b, nt),
        in_specs=[col(0), col(1), col(2), col(3),
                  const3(dmat.shape), const3(kws.shape), const3(qws.shape),
                  pl.BlockSpec((RET_HEADS, HEAD_DIM), lambda bi, ti: (0, 0))],
        out_specs=pl.BlockSpec((c, RET_W), lambda bi, ti: (bi * nt + ti, 0)),
        out_shape=jax.ShapeDtypeStruct((b * t, RET_W), BF16),
        scratch_shapes=[pltpu.VMEM((RET_HEADS, HEAD_DIM, HEAD_DIM), F32)],
        compiler_params=pltpu.CompilerParams(
            dimension_semantics=("parallel", "arbitrary"), vmem_limit_bytes=VMEM_LIMIT),
        name="retention",
    )(proj, proj, proj, proj, dmat, kws, qws, ret_norm_w)


def _gdn_kernel(qkv_ref, z_ref, sm_ref, cw_ref, al_ref, dtb_ref, nw_ref, tril_ref, o_ref,
                buf_ref, st_ref):
    c = GDN_C

    @pl.when(pl.program_id(1) == 0)
    def _():
        buf_ref[0:HALO, :] = jnp.zeros((HALO, buf_ref.shape[1]), F32)
        st_ref[...] = jnp.zeros_like(st_ref)

    buf_ref[HALO:HALO + c, :] = qkv_ref[...].astype(F32)

    def conv_act(col):
        sl = slice(col * HEAD_DIM, (col + 1) * HEAD_DIM)
        acc = cw_ref[GDN_CONV - 1:GDN_CONV, sl] * buf_ref[HALO:HALO + c, sl]
        for j in range(GDN_CONV - 1):
            sh = GDN_CONV - 1 - j
            acc = acc + cw_ref[j:j + 1, sl] * buf_ref[HALO - sh:HALO - sh + c, sl]
        return _silu(acc)

    sm = sm_ref[...]
    x = sm + dtb_ref[...]
    softplus = jnp.maximum(x, 0.0) + jnp.log1p(jnp.exp(-jnp.abs(x)))
    g = -jnp.exp(al_ref[...]) * softplus
    beta = _sigmoid(sm)
    tril = tril_ref[...]
    g1, g2, g3 = _split3(g)
    gc = _dot(tril, g1) + _dot(tril, g2) + _dot(tril, g3)
    gct = gc.T

    row = lax.broadcasted_iota(jnp.int32, (c, c), 0)
    coli = lax.broadcasted_iota(jnp.int32, (c, c), 1)
    causal = row >= coli
    strict = row > coli
    eye = jnp.where(row == coli, 1.0, 0.0).astype(F32)
    diag8 = strict & ((row >> 3) == (coli >> 3))
    merge_masks = [((row >> (lv + 1)) == (coli >> (lv + 1))) & ((row >> lv) != (coli >> lv)) & strict
                   for lv in range(3, 7)]

    for h in range(GDN_HEADS):
        qr = conv_act(h)
        kr = conv_act(GDN_HEADS + h)
        v = conv_act(2 * GDN_HEADS + h)
        q = qr * lax.rsqrt(jnp.sum(qr * qr, axis=-1, keepdims=True) + RMS_EPS) * SCALE
        k = kr * lax.rsqrt(jnp.sum(kr * kr, axis=-1, keepdims=True) + RMS_EPS)

        gcol = jnp.broadcast_to(gc[:, h:h + 1], (c, c))
        grow = gct[h:h + 1, :]
        dm = jnp.exp(jnp.where(causal, gcol - grow, NEG_BIG))
        bb = jnp.broadcast_to(beta[:, GDN_HEADS + h:GDN_HEADS + h + 1], (c, HEAD_DIM))
        eg = jnp.exp(gcol)
        glast = gcol[c - 1:c, :]
        ek = jnp.exp(glast - gcol)

        kb = k * bb
        k_b = k.astype(BF16)
        mm = jnp.where(strict, _dot_nt(kb.astype(BF16), k_b) * dm, 0.0)
        qk = _dot_nt(q.astype(BF16), k_b) * dm

        md = jnp.where(diag8, mm, 0.0)
        md_b = md.astype(BF16)
        xi = eye - md
        md2 = _dot(md_b, md_b)
        md2_b = md2.astype(BF16)
        xi = xi + _dot(md2_b, xi.astype(BF16))
        md4_b = _dot(md2_b, md2_b).astype(BF16)
        xi = xi + _dot(md4_b, xi.astype(BF16))
        for mask in merge_masks:
            mo_b = jnp.where(mask, mm, 0.0).astype(BF16)
            xi_b = xi.astype(BF16)
            xi = xi - _dot(_dot(xi_b, mo_b).astype(BF16), xi_b)
        xc_b = (xi - eye).astype(BF16)

        rhs = jnp.concatenate([kb * eg, v * bb], axis=-1)
        wu = rhs + _dot(xc_b, rhs.astype(BF16))
        w = wu[:, :HEAD_DIM]
        u = wu[:, HEAD_DIM:]

        state = st_ref[h]
        state_b = state.astype(BF16)
        v_new = u - _dot(w.astype(BF16), state_b)
        v_new_b = v_new.astype(BF16)
        o = _dot((q * eg).astype(BF16), state_b) + _dot(qk.astype(BF16), v_new_b)
        st_ref[h] = state * jnp.exp(glast) + _dot_tn((k * ek).astype(BF16), v_new_b)

        sl = slice(h * HEAD_DIM, (h + 1) * HEAD_DIM)
        ms = jnp.mean(o * o, axis=-1, keepdims=True)
        y = o * lax.rsqrt(ms + RMS_EPS) * nw_ref[...]
        o_ref[:, sl] = (y * _silu(z_ref[:, sl].astype(F32))).astype(o_ref.dtype)

    buf_ref[0:HALO, :] = buf_ref[c:c + HALO, :]


def _gdn(proj, small, conv_w, a_log, dt_bias, gdn_norm_w, b, t):
    c = GDN_C
    nt = t // c
    qkv_w = 3 * GDN_W
    pad = SMALL_W - GDN_HEADS
    al_row = jnp.pad(a_log.astype(F32), (0, pad)).reshape(1, SMALL_W)
    dtb_row = jnp.pad(dt_bias.astype(F32), (0, pad)).reshape(1, SMALL_W)
    tril = jnp.asarray(np.tril(np.ones((c, c), np.float32)), BF16)
    return pl.pallas_call(
        _gdn_kernel,
        grid=(b, nt),
        in_specs=[
            pl.BlockSpec((c, qkv_w), lambda bi, ti: (bi * nt + ti, GDN_OFF // qkv_w)),
            pl.BlockSpec((c, GDN_W), lambda bi, ti: (bi * nt + ti, (GDN_OFF + qkv_w) // GDN_W)),
            pl.BlockSpec((c, SMALL_W), lambda bi, ti: (bi * nt + ti, 0)),
            pl.BlockSpec((GDN_CONV, qkv_w), lambda bi, ti: (0, 0)),
            pl.BlockSpec((1, SMALL_W), lambda bi, ti: (0, 0)),
            pl.BlockSpec((1, SMALL_W), lambda bi, ti: (0, 0)),
            pl.BlockSpec((1, HEAD_DIM), lambda bi, ti: (0, 0)),
            pl.BlockSpec((c, c), lambda bi, ti: (0, 0)),
        ],
        out_specs=pl.BlockSpec((c, GDN_W), lambda bi, ti: (bi * nt + ti, 0)),
        out_shape=jax.ShapeDtypeStruct((b * t, GDN_W), BF16),
        scratch_shapes=[pltpu.VMEM((HALO + c, qkv_w), F32),
                        pltpu.VMEM((GDN_HEADS, HEAD_DIM, HEAD_DIM), F32)],
        compiler_params=pltpu.CompilerParams(
            dimension_semantics=("parallel", "arbitrary"), vmem_limit_bytes=VMEM_LIMIT),
        name="gdn",
    )(proj, proj, small, conv_w, al_row, dtb_row, gdn_norm_w.reshape(1, HEAD_DIM), tril)


def _moba_key_features(t):
    pos = np.arange(t)
    blk = pos // MOBA_BLOCK
    off = pos % MOBA_BLOCK
    f = np.zeros((t, HEAD_DIM), np.float32)
    f[pos, blk] = 1.0
    f[:, SEL_LANES] = 1.0
    for p in range(3):
        f[:, SEL_LANES + 1 + p] = blk
        f[:, SEL_LANES + 4 + p] = off // 16
        f[:, SEL_LANES + 7 + p] = off % 16
    return jnp.asarray(f, BF16)


def _moba_query_features():
    hh = MOBA_HEADS
    slopes = np.exp2(-8.0 * np.arange(1, hh + 1, dtype=np.float64) / hh).astype(np.float32)
    s = jnp.asarray(slopes, F32)
    s1 = s.astype(BF16).astype(F32)
    s2 = (s - s1).astype(BF16).astype(F32)
    s3 = (s - s1 - s2).astype(BF16).astype(F32)
    f = jnp.zeros((hh, HEAD_DIM), F32)
    f = f.at[:, SEL_LANES].set(-s * MOBA_BLOCK)
    for p, sp in enumerate((s1, s2, s3)):
        f = f.at[:, SEL_LANES + 1 + p].set(sp * MOBA_BLOCK)
        f = f.at[:, SEL_LANES + 4 + p].set(sp * 16.0)
        f = f.at[:, SEL_LANES + 7 + p].set(sp)
    return jnp.broadcast_to(f[:, None, :], (hh, 8, HEAD_DIM))


def _moba_prep_kernel(k_ref, nw_ref, pf_ref, ka_ref, km_ref):
    for h in range(MOBA_HEADS):
        sl = slice(h * HEAD_DIM, (h + 1) * HEAD_DIM)
        k = k_ref[:, sl].astype(F32)
        ms = jnp.mean(k * k, axis=-1, keepdims=True)
        kn = k * lax.rsqrt(ms + RMS_EPS) * nw_ref[...]
        ka_ref[0, h, :, 0:HEAD_DIM] = kn.astype(BF16)
        ka_ref[0, h, :, HEAD_DIM:2 * HEAD_DIM] = pf_ref[...]
        km_ref[0, 0, :, sl] = jnp.mean(kn, axis=0, keepdims=True)


def _moba_prep(proj, k_norm_w, key_feat, b, t):
    blk = MOBA_BLOCK
    nb = t // blk
    return pl.pallas_call(
        _moba_prep_kernel,
        grid=(b, nb),
        in_specs=[
            pl.BlockSpec((blk, MOBA_W), lambda bi, ti: (bi * nb + ti, (MOBA_OFF + MOBA_W) // MOBA_W)),
            pl.BlockSpec((1, HEAD_DIM), lambda bi, ti: (0, 0)),
            pl.BlockSpec((blk, HEAD_DIM), lambda bi, ti: (ti, 0)),
        ],
        out_specs=[
            pl.BlockSpec((1, MOBA_HEADS, blk, 2 * HEAD_DIM), lambda bi, ti: (bi, 0, ti, 0)),
            pl.BlockSpec((1, 1, 1, MOBA_W), lambda bi, ti: (bi, ti, 0, 0)),
        ],
        out_shape=[
            jax.ShapeDtypeStruct((b, MOBA_HEADS, t, 2 * HEAD_DIM), BF16),
            jax.ShapeDtypeStruct((b, nb, 1, MOBA_W), F32),
        ],
        compiler_params=pltpu.CompilerParams(
            dimension_semantics=("parallel", "parallel"), vmem_limit_bytes=VMEM_LIMIT),
        name="moba_prep",
    )(proj, k_norm_w.reshape(1, HEAD_DIM), key_feat)


def _moba_attn_kernel(q_ref, ka_ref, v_ref, z_ref, km_ref, qnw_ref, qf_ref, o_ref,
                      qa_ref, m_ref, l_ref, acc_ref):
    blk = MOBA_BLOCK
    qi = pl.program_id(2)

    q = q_ref[...].astype(F32)
    ms = jnp.mean(q * q, axis=-1, keepdims=True)
    qn = q * lax.rsqrt(ms + RMS_EPS) * qnw_ref[...]

    km = km_ref[0]
    q1 = qn.astype(BF16)
    q2 = (qn - q1.astype(F32)).astype(BF16)
    k1 = km.astype(BF16)
    k2 = (km - k1.astype(F32)).astype(BF16)
    gate = _dot_nt(q1, k1) + _dot_nt(q1, k2) + _dot_nt(q2, k1)
    lane = lax.broadcasted_iota(jnp.int32, (blk, HEAD_DIM), 1)
    lane_f = lane.astype(F32)
    gate = jnp.where(lane < qi, gate, -jnp.inf)
    taken = jnp.zeros((blk, HEAD_DIM), F32)
    for _ in range(MOBA_TOP_K):
        gm = jnp.where(taken > 0.0, -jnp.inf, gate)
        mx = jnp.max(gm, axis=-1, keepdims=True)
        cand = jnp.where(taken > 0.0, float(HEAD_DIM), jnp.where(gm == mx, lane_f, float(HEAD_DIM)))
        idx = jnp.min(cand, axis=-1, keepdims=True)
        taken = jnp.where(lane_f == idx, 1.0, taken)
    sel_bias = jnp.where(lane == qi, 0.0,
                         jnp.where(lane < qi, jnp.where(taken > 0.0, 0.0, NEG_BIG), NEG_BIG))
    coef = qf_ref[0, 0:1, :] * jnp.where(lane == SEL_LANES, qi.astype(F32), 1.0)
    extra = jnp.where(lane < SEL_LANES, sel_bias, coef)
    qa_ref[:, 0:HEAD_DIM] = (qn * SCALE).astype(BF16)
    qa_ref[:, HEAD_DIM:2 * HEAD_DIM] = extra.astype(BF16)

    off = pl.multiple_of(qi * blk, blk)
    s = _dot_nt(qa_ref[...], ka_ref[0, 0, pl.ds(off, blk), :])
    row = lax.broadcasted_iota(jnp.int32, (blk, blk), 0)
    colk = lax.broadcasted_iota(jnp.int32, (blk, blk), 1)
    s = jnp.where(colk <= row, s, NEG_BIG)
    m0 = jnp.max(s, axis=-1, keepdims=True)
    p = jnp.exp(s - m0)
    m_ref[...] = jnp.broadcast_to(m0, m_ref.shape)
    l_ref[...] = jnp.broadcast_to(jnp.sum(p, axis=-1, keepdims=True), l_ref.shape)
    acc_ref[...] = _dot(p.astype(BF16), v_ref[pl.ds(off, blk), :])

    def body(j, carry):
        offj = pl.multiple_of(j * blk, blk)
        sj = _dot_nt(qa_ref[...], ka_ref[0, 0, pl.ds(offj, blk), :])
        m_prev = m_ref[...]
        m_new = jnp.maximum(m_prev, jnp.max(sj, axis=-1, keepdims=True))
        alpha = jnp.exp(m_prev - m_new)
        pj = jnp.exp(sj - jnp.concatenate([m_new, m_new], axis=-1))
        l_ref[...] = alpha * l_ref[...] + jnp.sum(pj, axis=-1, keepdims=True)
        acc_ref[...] = alpha * acc_ref[...] + _dot(pj.astype(BF16), v_ref[pl.ds(offj, blk), :])
        m_ref[...] = m_new
        return carry

    lax.fori_loop(0, qi, body, 0)

    o = acc_ref[...] / l_ref[...]
    o_ref[...] = (o * _silu(z_ref[...].astype(F32))).astype(o_ref.dtype)


def _moba_attn(proj, k_aug, k_mean, q_norm_w, q_feat, b, t):
    blk = MOBA_BLOCK
    nb = t // blk
    qc = MOBA_OFF // HEAD_DIM
    vc = (MOBA_OFF + 2 * MOBA_W) // HEAD_DIM
    zc = (MOBA_OFF + 3 * MOBA_W) // HEAD_DIM
    return pl.pallas_call(
        _moba_attn_kernel,
        grid=(b, MOBA_HEADS, nb),
        in_specs=[
            pl.BlockSpec((blk, HEAD_DIM), lambda bi, hi, qi: (bi * nb + qi, qc + hi)),
            pl.BlockSpec((1, 1, t, 2 * HEAD_DIM), lambda bi, hi, qi: (bi, hi, 0, 0)),
            pl.BlockSpec((t, HEAD_DIM), lambda bi, hi, qi: (bi, vc + hi)),
            pl.BlockSpec((blk, HEAD_DIM), lambda bi, hi, qi: (bi * nb + qi, zc + hi)),
            pl.BlockSpec((1, HEAD_DIM, HEAD_DIM), lambda bi, hi, qi: (bi, 0, hi)),
            pl.BlockSpec((1, HEAD_DIM), lambda bi, hi, qi: (0, 0)),
            pl.BlockSpec((1, 8, HEAD_DIM), lambda bi, hi, qi: (hi, 0, 0)),
        ],
        out_specs=pl.BlockSpec((blk, HEAD_DIM), lambda bi, hi, qi: (bi * nb + qi, hi)),
        out_shape=jax.ShapeDtypeStruct((b * t, MOBA_W), BF16),
        scratch_shapes=[pltpu.VMEM((blk, 2 * HEAD_DIM), BF16),
                        pltpu.VMEM((blk, HEAD_DIM), F32),
                        pltpu.VMEM((blk, HEAD_DIM), F32),
                        pltpu.VMEM((blk, HEAD_DIM), F32)],
        compiler_params=pltpu.CompilerParams(
            dimension_semantics=("parallel", "parallel", "arbitrary"), vmem_limit_bytes=VMEM_LIMIT),
        name="moba_attn",
    )(proj, k_aug, proj, proj, k_mean, q_norm_w.reshape(1, HEAD_DIM), q_feat)


def _layer(x2d, b, t, norm_w, w_main, w_small, conv_w, a_log, dt_bias, ret_norm_w, gdn_norm_w,
           q_norm_w, k_norm_w, w_out, key_feat, q_feat):
    nb = t // MOBA_BLOCK
    proj, small = _in_proj(x2d, norm_w, w_main, w_small)
    y_ret = _retention(proj, ret_norm_w, b, t)
    y_gdn = _gdn(proj, small, conv_w, a_log, dt_bias, gdn_norm_w, b, t)
    k_aug, k_mean = _moba_prep(proj, k_norm_w, key_feat, b, t)
    k_mean = jnp.pad(k_mean.reshape(b, nb, MOBA_W), ((0, 0), (0, HEAD_DIM - nb), (0, 0)))
    y_moba = _moba_attn(proj, k_aug, k_mean, q_norm_w, q_feat, b, t)
    return _out_proj(x2d, y_ret, y_gdn, y_moba, w_out)


def kernel(x, norm_w, w_in, conv_w, a_log, dt_bias, ret_norm_w, gdn_norm_w, q_norm_w, k_norm_w, w_out):
    b, t, d = x.shape
    depth = w_in.shape[0]
    assert t % MOBA_BLOCK == 0 and t // MOBA_BLOCK <= SEL_LANES
    gdn_lo = 4 * RET_W
    small_lo = gdn_lo + 4 * GDN_W
    moba_lo = small_lo + 2 * GDN_HEADS
    w_main = jnp.concatenate(
        [w_in[:, :, gdn_lo:small_lo], w_in[:, :, moba_lo:], w_in[:, :, :gdn_lo]], axis=-1).astype(BF16)
    w_small = jnp.pad(w_in[:, :, small_lo:moba_lo],
                      ((0, 0), (0, 0), (0, SMALL_W - 2 * GDN_HEADS))).astype(BF16)
    w_out_b = w_out.astype(BF16)
    key_feat = _moba_key_features(t)
    q_feat = _moba_query_features()
    x2d = x.reshape(b * t, d)
    for layer in range(depth):
        x2d = _layer(x2d, b, t, norm_w[layer], w_main[layer], w_small[layer], conv_w[layer],
                     a_log[layer], dt_bias[layer], ret_norm_w[layer], gdn_norm_w[layer],
                     q_norm_w[layer], k_norm_w[layer], w_out_b[layer], key_feat, q_feat)
    return x2d.reshape(b, t, d)
```

```python
import functools

import numpy as np
import jax
import jax.numpy as jnp
from jax import lax
from jax.experimental import pallas as pl
from jax.experimental.pallas import tpu as pltpu

F32 = jnp.float32
BF16 = jnp.bfloat16

HEAD_DIM = 128
RET_HEADS = 4
GDN_HEADS = 6
MOBA_HEADS = 6
RET_W = RET_HEADS * HEAD_DIM
GDN_W = GDN_HEADS * HEAD_DIM
MOBA_W = MOBA_HEADS * HEAD_DIM
GDN_CONV = 4
MOBA_BLOCK = 256
MOBA_TOP_K = 3
RMS_EPS = 1e-6
SCALE = HEAD_DIM ** -0.5

GDN_OFF = 0
MOBA_OFF = 4 * GDN_W
RET_OFF = 4 * GDN_W + 4 * MOBA_W
MAIN_W = RET_OFF + 4 * RET_W
SMALL_W = 128

RET_C = 256
GDN_C = 128
HALO = 8

NEG_BIG = -1e30
SEL_LANES = 32
VMEM_LIMIT = 56 * 1024 * 1024


def _dot(a, b):
    return jnp.dot(a, b, preferred_element_type=F32)


def _dot_nt(a, b):
    return lax.dot_general(a, b, (((1,), (1,)), ((), ())), preferred_element_type=F32)


def _dot_tn(a, b):
    return lax.dot_general(a, b, (((0,), (0,)), ((), ())), preferred_element_type=F32)


def _sigmoid(x):
    return 1.0 / (1.0 + jnp.exp(-x))


def _silu(x):
    return x * _sigmoid(x)


def _split3(x):
    x1 = x.astype(BF16)
    r = x - x1.astype(F32)
    x2 = r.astype(BF16)
    x3 = (r - x2.astype(F32)).astype(BF16)
    return x1, x2, x3


def _in_proj_kernel(x_ref, nw_ref, w_ref, ws_ref, o_ref, os_ref, h_ref):
    @pl.when(pl.program_id(1) == 0)
    def _():
        x = x_ref[...]
        ms = jnp.mean(x * x, axis=-1, keepdims=True)
        h = (x * lax.rsqrt(ms + RMS_EPS) * nw_ref[...]).astype(BF16)
        h_ref[...] = h
        os_ref[...] = _dot(h, ws_ref[...])

    o_ref[...] = _dot(h_ref[...], w_ref[...]).astype(o_ref.dtype)


def _in_proj(x2d, norm_w, w_main, w_small, tm=1024, tn=512):
    m, d = x2d.shape
    n = w_main.shape[1]
    return pl.pallas_call(
        _in_proj_kernel,
        grid=(m // tm, n // tn),
        in_specs=[
            pl.BlockSpec((tm, d), lambda i, j: (i, 0)),
            pl.BlockSpec((1, d), lambda i, j: (0, 0)),
            pl.BlockSpec((d, tn), lambda i, j: (0, j)),
            pl.BlockSpec((d, SMALL_W), lambda i, j: (0, 0)),
        ],
        out_specs=[
            pl.BlockSpec((tm, tn), lambda i, j: (i, j)),
            pl.BlockSpec((tm, SMALL_W), lambda i, j: (i, 0)),
        ],
        out_shape=[
            jax.ShapeDtypeStruct((m, n), BF16),
            jax.ShapeDtypeStruct((m, SMALL_W), F32),
        ],
        scratch_shapes=[pltpu.VMEM((tm, d), BF16)],
        compiler_params=pltpu.CompilerParams(
            dimension_semantics=("parallel", "arbitrary"), vmem_limit_bytes=VMEM_LIMIT),
        name="in_proj",
    )(x2d, norm_w.reshape(1, d), w_main, w_small)


def _out_proj_kernel(x_ref, yr_ref, yg_ref, ym_ref, w_ref, o_ref, *, tn):
    n = o_ref.shape[1]
    yr = yr_ref[...]
    yg = yg_ref[...]
    ym = ym_ref[...]
    for c in range(0, n, tn):
        acc = x_ref[:, c:c + tn]
        acc = acc + _dot(yr, w_ref[0:RET_W, c:c + tn])
        acc = acc + _dot(yg, w_ref[RET_W:RET_W + GDN_W, c:c + tn])
        acc = acc + _dot(ym, w_ref[RET_W + GDN_W:RET_W + GDN_W + MOBA_W, c:c + tn])
        o_ref[:, c:c + tn] = acc


def _out_proj(x2d, y_ret, y_gdn, y_moba, w_out, tm=512, tn=512):
    m, d = x2d.shape
    return pl.pallas_call(
        functools.partial(_out_proj_kernel, tn=tn),
        grid=(m // tm,),
        in_specs=[
            pl.BlockSpec((tm, d), lambda i: (i, 0)),
            pl.BlockSpec((tm, RET_W), lambda i: (i, 0)),
            pl.BlockSpec((tm, GDN_W), lambda i: (i, 0)),
            pl.BlockSpec((tm, MOBA_W), lambda i: (i, 0)),
            pl.BlockSpec(w_out.shape, lambda i: (0, 0)),
        ],
        out_specs=pl.BlockSpec((tm, d), lambda i: (i, 0)),
        out_shape=jax.ShapeDtypeStruct((m, d), F32),
        compiler_params=pltpu.CompilerParams(
            dimension_semantics=("parallel",), vmem_limit_bytes=VMEM_LIMIT),
        name="out_proj",
    )(x2d, y_ret, y_gdn, y_moba, w_out)


def _retention_consts(c):
    h = np.arange(RET_HEADS, dtype=np.float64)
    log_gamma = np.log1p(-np.exp2(-5.0 - h))
    pos = np.arange(c, dtype=np.float64)
    diff = pos[:, None] - pos[None, :]
    dmat = np.where(diff >= 0, np.exp(np.maximum(diff, 0.0) * log_gamma[:, None, None]), 0.0) * SCALE
    kws = np.exp((c - 1 - pos)[None, :] * log_gamma[:, None])
    qws = np.exp((pos + 1.0)[None, :] * log_gamma[:, None]) * SCALE
    kws = np.broadcast_to(kws[:, :, None], (RET_HEADS, c, HEAD_DIM))
    qws = np.broadcast_to(qws[:, :, None], (RET_HEADS, c, HEAD_DIM))
    chunk_decay = [float(v) for v in np.exp(c * log_gamma)]
    return (jnp.asarray(dmat, F32), jnp.asarray(kws, F32), jnp.asarray(qws, F32), chunk_decay)


def _retention_kernel(q_ref, k_ref, v_ref, g_ref, dm_ref, kws_ref, qws_ref, nw_ref, o_ref, st_ref,
                      *, chunk_decay):
    @pl.when(pl.program_id(1) == 0)
    def _():
        st_ref[...] = jnp.zeros_like(st_ref)

    for h in range(RET_HEADS):
        sl = slice(h * HEAD_DIM, (h + 1) * HEAD_DIM)
        q = q_ref[:, sl]
        k = k_ref[:, sl]
        v = v_ref[:, sl]
        s = _dot_nt(q, k) * dm_ref[h]
        intra = _dot(s.astype(BF16), v)
        state = st_ref[h]
        qw = (q.astype(F32) * qws_ref[h]).astype(BF16)
        o = intra + _dot(qw, state.astype(BF16))
        kw = (k.astype(F32) * kws_ref[h]).astype(BF16)
        st_ref[h] = state * chunk_decay[h] + _dot_tn(kw, v)
        ms = jnp.mean(o * o, axis=-1, keepdims=True)
        y = o * lax.rsqrt(ms + RMS_EPS) * nw_ref[h:h + 1, :]
        o_ref[:, sl] = (y * _silu(g_ref[:, sl].astype(F32))).astype(o_ref.dtype)


def _retention(proj, ret_norm_w, b, t):
    c = RET_C
    nt = t // c
    dmat, kws, qws, chunk_decay = _retention_consts(c)
    base = RET_OFF // RET_W

    def col(ci):
        return pl.BlockSpec((c, RET_W), lambda bi, ti: (bi * nt + ti, base + ci))

    def const3(shape):
        return pl.BlockSpec(shape, lambda bi, ti: (0, 0, 0))

    return pl.pallas_call(
        functools.partial(_retention_kernel, chunk_decay=chunk_decay),
        grid=(b, nt),
        in_specs=[col(0), col(1), col(2), col(3),
                  const3(dmat.shape), const3(kws.shape), const3(qws.shape),
                  pl.BlockSpec((RET_HEADS, HEAD_DIM), lambda bi, ti: (0, 0))],
        out_specs=pl.BlockSpec((c, RET_W), lambda bi, ti: (bi * nt + ti, 0)),
        out_shape=jax.ShapeDtypeStruct((b * t, RET_W), BF16),
        scratch_shapes=[pltpu.VMEM((RET_HEADS, HEAD_DIM, HEAD_DIM), F32)],
        compiler_params=pltpu.CompilerParams(
            dimension_semantics=("parallel", "arbitrary"), vmem_limit_bytes=VMEM_LIMIT),
        name="retention",
    )(proj, proj, proj, proj, dmat, kws, qws, ret_norm_w)


def _gdn_kernel(qkv_ref, z_ref, sm_ref, cw_ref, al_ref, dtb_ref, nw_ref, tril_ref, o_ref,
                buf_ref, st_ref):
    c = GDN_C

    @pl.when(pl.program_id(1) == 0)
    def _():
        buf_ref[0:HALO, :] = jnp.zeros((HALO, buf_ref.shape[1]), F32)
        st_ref[...] = jnp.zeros_like(st_ref)

    buf_ref[HALO:HALO + c, :] = qkv_ref[...].astype(F32)

    def conv_act(col):
        sl = slice(col * HEAD_DIM, (col + 1) * HEAD_DIM)
        acc = cw_ref[GDN_CONV - 1:GDN_CONV, sl] * buf_ref[HALO:HALO + c, sl]
        for j in range(GDN_CONV - 1):
            sh = GDN_CONV - 1 - j
            acc = acc + cw_ref[j:j + 1, sl] * buf_ref[HALO - sh:HALO - sh + c, sl]
        return _silu(acc)

    sm = sm_ref[...]
    x = sm + dtb_ref[...]
    softplus = jnp.maximum(x, 0.0) + jnp.log1p(jnp.exp(-jnp.abs(x)))
    g = -jnp.exp(al_ref[...]) * softplus
    beta = _sigmoid(sm)
    tril = tril_ref[...]
    g1, g2, g3 = _split3(g)
    gc = _dot(tril, g1) + _dot(tril, g2) + _dot(tril, g3)
    gct = gc.T

    row = lax.broadcasted_iota(jnp.int32, (c, c), 0)
    coli = lax.broadcasted_iota(jnp.int32, (c, c), 1)
    causal = row >= coli
    strict = row > coli
    eye = jnp.where(row == coli, 1.0, 0.0).astype(F32)
    diag8 = strict & ((row >> 3) == (coli >> 3))
    merge_masks = [((row >> (lv + 1)) == (coli >> (lv + 1))) & ((row >> lv) != (coli >> lv)) & strict
                   for lv in range(3, 7)]

    heads = range(GDN_HEADS)
    d = HEAD_DIM

    def l2n(t):
        return t * lax.rsqrt(jnp.sum(t * t, axis=-1, keepdims=True) + RMS_EPS)

    q = [l2n(conv_act(h)) * SCALE for h in heads]
    k = [l2n(conv_act(GDN_HEADS + h)) for h in heads]
    v = [conv_act(2 * GDN_HEADS + h) for h in heads]
    gcol = [jnp.broadcast_to(gc[:, h:h + 1], (c, c)) for h in heads]
    dm = [jnp.exp(jnp.where(causal, gcol[h] - gct[h:h + 1, :], NEG_BIG)) for h in heads]
    bb = [jnp.broadcast_to(beta[:, GDN_HEADS + h:GDN_HEADS + h + 1], (c, d)) for h in heads]
    eg = [jnp.exp(gcol[h]) for h in heads]
    glast = [gcol[h][c - 1:c, :] for h in heads]
    ek = [jnp.exp(glast[h] - gcol[h]) for h in heads]
    kb = [k[h] * bb[h] for h in heads]
    k_b = [k[h].astype(BF16) for h in heads]

    sc = [_dot_nt(jnp.concatenate([kb[h], q[h]], axis=0).astype(BF16), k_b[h]) for h in heads]
    mm = [jnp.where(strict, sc[h][:c] * dm[h], 0.0) for h in heads]
    qk = [sc[h][c:] * dm[h] for h in heads]

    md = [jnp.where(diag8, mm[h], 0.0) for h in heads]
    md_b = [md[h].astype(BF16) for h in heads]
    x0 = [eye - md[h] for h in heads]
    md2_b = [_dot(md_b[h], md_b[h]).astype(BF16) for h in heads]
    t1 = [_dot(md2_b[h], jnp.concatenate([x0[h].astype(BF16), md2_b[h]], axis=-1)) for h in heads]
    x1 = [x0[h] + t1[h][:, :c] for h in heads]
    md4_b = [t1[h][:, c:].astype(BF16) for h in heads]
    xi = [x1[h] + _dot(md4_b[h], x1[h].astype(BF16)) for h in heads]
    for mask in merge_masks:
        mo_b = [jnp.where(mask, mm[h], 0.0).astype(BF16) for h in heads]
        xi_b = [xi[h].astype(BF16) for h in heads]
        y_b = [_dot(xi_b[h], mo_b[h]).astype(BF16) for h in heads]
        xi = [xi[h] - _dot(y_b[h], xi_b[h]) for h in heads]
    xc_b = [(xi[h] - eye).astype(BF16) for h in heads]

    rhs = [jnp.concatenate([kb[h] * eg[h], v[h] * bb[h]], axis=-1) for h in heads]
    wu = [rhs[h] + _dot(xc_b[h], rhs[h].astype(BF16)) for h in heads]

    state = [st_ref[h] for h in heads]
    state_b = [state[h].astype(BF16) for h in heads]
    ws = [_dot(jnp.concatenate([wu[h][:, :d], q[h] * eg[h]], axis=0).astype(BF16), state_b[h])
          for h in heads]
    v_new_b = [(wu[h][:, d:] - ws[h][:c]).astype(BF16) for h in heads]
    o = [ws[h][c:] + _dot(qk[h].astype(BF16), v_new_b[h]) for h in heads]
    for h in heads:
        st_ref[h] = state[h] * jnp.exp(glast[h]) + _dot_tn((k[h] * ek[h]).astype(BF16), v_new_b[h])

    for h in heads:
        sl = slice(h * d, (h + 1) * d)
        ms = jnp.mean(o[h] * o[h], axis=-1, keepdims=True)
        y = o[h] * lax.rsqrt(ms + RMS_EPS) * nw_ref[...]
        o_ref[:, sl] = (y * _silu(z_ref[:, sl].astype(F32))).astype(o_ref.dtype)

    buf_ref[0:HALO, :] = buf_ref[c:c + HALO, :]


def _gdn(proj, small, conv_w, a_log, dt_bias, gdn_norm_w, b, t):
    c = GDN_C
    nt = t // c
    qkv_w = 3 * GDN_W
    pad = SMALL_W - GDN_HEADS
    al_row = jnp.pad(a_log.astype(F32), (0, pad)).reshape(1, SMALL_W)
    dtb_row = jnp.pad(dt_bias.astype(F32), (0, pad)).reshape(1, SMALL_W)
    tril = jnp.asarray(np.tril(np.ones((c, c), np.float32)), BF16)
    return pl.pallas_call(
        _gdn_kernel,
        grid=(b, nt),
        in_specs=[
            pl.BlockSpec((c, qkv_w), lambda bi, ti: (bi * nt + ti, GDN_OFF // qkv_w)),
            pl.BlockSpec((c, GDN_W), lambda bi, ti: (bi * nt + ti, (GDN_OFF + qkv_w) // GDN_W)),
            pl.BlockSpec((c, SMALL_W), lambda bi, ti: (bi * nt + ti, 0)),
            pl.BlockSpec((GDN_CONV, qkv_w), lambda bi, ti: (0, 0)),
            pl.BlockSpec((1, SMALL_W), lambda bi, ti: (0, 0)),
            pl.BlockSpec((1, SMALL_W), lambda bi, ti: (0, 0)),
            pl.BlockSpec((1, HEAD_DIM), lambda bi, ti: (0, 0)),
            pl.BlockSpec((c, c), lambda bi, ti: (0, 0)),
        ],
        out_specs=pl.BlockSpec((c, GDN_W), lambda bi, ti: (bi * nt + ti, 0)),
        out_shape=jax.ShapeDtypeStruct((b * t, GDN_W), BF16),
        scratch_shapes=[pltpu.VMEM((HALO + c, qkv_w), F32),
                        pltpu.VMEM((GDN_HEADS, HEAD_DIM, HEAD_DIM), F32)],
        compiler_params=pltpu.CompilerParams(
            dimension_semantics=("parallel", "arbitrary"), vmem_limit_bytes=VMEM_LIMIT),
        name="gdn",
    )(proj, proj, small, conv_w, al_row, dtb_row, gdn_norm_w.reshape(1, HEAD_DIM), tril)


def _moba_key_features(t):
    pos = np.arange(t)
    blk = pos // MOBA_BLOCK
    off = pos % MOBA_BLOCK
    f = np.zeros((t, HEAD_DIM), np.float32)
    f[pos, blk] = 1.0
    f[:, SEL_LANES] = 1.0
    for p in range(3):
        f[:, SEL_LANES + 1 + p] = blk
        f[:, SEL_LANES + 4 + p] = off // 16
        f[:, SEL_LANES + 7 + p] = off % 16
    return jnp.asarray(f, BF16)


def _moba_query_features():
    hh = MOBA_HEADS
    slopes = np.exp2(-8.0 * np.arange(1, hh + 1, dtype=np.float64) / hh).astype(np.float32)
    s = jnp.asarray(slopes, F32)
    s1 = s.astype(BF16).astype(F32)
    s2 = (s - s1).astype(BF16).astype(F32)
    s3 = (s - s1 - s2).astype(BF16).astype(F32)
    f = jnp.zeros((hh, HEAD_DIM), F32)
    f = f.at[:, SEL_LANES].set(-s * MOBA_BLOCK)
    for p, sp in enumerate((s1, s2, s3)):
        f = f.at[:, SEL_LANES + 1 + p].set(sp * MOBA_BLOCK)
        f = f.at[:, SEL_LANES + 4 + p].set(sp * 16.0)
        f = f.at[:, SEL_LANES + 7 + p].set(sp)
    return f


def _moba_prep_kernel(k_ref, nw_ref, pf_ref, ka_ref, km_ref):
    for h in range(MOBA_HEADS):
        sl = slice(h * HEAD_DIM, (h + 1) * HEAD_DIM)
        k = k_ref[:, sl].astype(F32)
        ms = jnp.mean(k * k, axis=-1, keepdims=True)
        kn = k * lax.rsqrt(ms + RMS_EPS) * nw_ref[...]
        ka_ref[0, h, :, 0:HEAD_DIM] = kn.astype(BF16)
        ka_ref[0, h, :, HEAD_DIM:2 * HEAD_DIM] = pf_ref[...]
        km_ref[0, 0, :, sl] = jnp.mean(kn, axis=0, keepdims=True)


def _moba_prep(proj, k_norm_w, key_feat, b, t):
    blk = MOBA_BLOCK
    nb = t // blk
    return pl.pallas_call(
        _moba_prep_kernel,
        grid=(b, nb),
        in_specs=[
            pl.BlockSpec((blk, MOBA_W), lambda bi, ti: (bi * nb + ti, (MOBA_OFF + MOBA_W) // MOBA_W)),
            pl.BlockSpec((1, HEAD_DIM), lambda bi, ti: (0, 0)),
            pl.BlockSpec((blk, HEAD_DIM), lambda bi, ti: (ti, 0)),
        ],
        out_specs=[
            pl.BlockSpec((1, MOBA_HEADS, blk, 2 * HEAD_DIM), lambda bi, ti: (bi, 0, ti, 0)),
            pl.BlockSpec((1, 1, 1, MOBA_W), lambda bi, ti: (bi, ti, 0, 0)),
        ],
        out_shape=[
            jax.ShapeDtypeStruct((b, MOBA_HEADS, t, 2 * HEAD_DIM), BF16),
            jax.ShapeDtypeStruct((b, nb, 1, MOBA_W), F32),
        ],
        compiler_params=pltpu.CompilerParams(
            dimension_semantics=("parallel", "parallel"), vmem_limit_bytes=VMEM_LIMIT),
        name="moba_prep",
    )(proj, k_norm_w.reshape(1, HEAD_DIM), key_feat)


def _moba_qprep_kernel(q_ref, km_ref, qnw_ref, qf_ref, qa_ref):
    tq = q_ref.shape[0]
    lane = lax.broadcasted_iota(jnp.int32, (tq, HEAD_DIM), 1)
    lane_f = lane.astype(F32)
    rowi = lax.broadcasted_iota(jnp.int32, (tq, HEAD_DIM), 0)
    own = (pl.program_id(1) * tq + rowi) // MOBA_BLOCK
    past = lane < own
    own_coef = jnp.where(lane == SEL_LANES, own.astype(F32), 1.0)
    heads = range(MOBA_HEADS)
    hsl = [slice(h * HEAD_DIM, (h + 1) * HEAD_DIM) for h in heads]

    def qnorm(h):
        q = q_ref[:, hsl[h]].astype(F32)
        ms = jnp.mean(q * q, axis=-1, keepdims=True)
        return q * lax.rsqrt(ms + RMS_EPS) * qnw_ref[...]

    qn = [qnorm(h) for h in heads]
    for h in heads:
        qa_ref[0, h, :, 0:HEAD_DIM] = (qn[h] * SCALE).astype(BF16)

    def gate_scores(h):
        km = km_ref[0, :, hsl[h]]
        q1 = qn[h].astype(BF16)
        q2 = (qn[h] - q1.astype(F32)).astype(BF16)
        k1 = km.astype(BF16)
        k2 = (km - k1.astype(F32)).astype(BF16)
        return jnp.where(past, _dot_nt(q1, k1) + _dot_nt(q1, k2) + _dot_nt(q2, k1), -jnp.inf)

    gate = [gate_scores(h) for h in heads]
    taken = [jnp.zeros((tq, HEAD_DIM), F32) for _ in heads]
    for _ in range(MOBA_TOP_K):
        gm = [jnp.where(taken[h] > 0.0, -jnp.inf, gate[h]) for h in heads]
        mx = [jnp.max(gm[h], axis=-1, keepdims=True) for h in heads]
        cand = [jnp.where(taken[h] > 0.0, float(HEAD_DIM),
                          jnp.where(gm[h] == mx[h], lane_f, float(HEAD_DIM))) for h in heads]
        idx = [jnp.min(cand[h], axis=-1, keepdims=True) for h in heads]
        taken = [jnp.where(lane_f == idx[h], 1.0, taken[h]) for h in heads]
    for h in heads:
        sel_bias = jnp.where(lane == own, 0.0,
                             jnp.where(past, jnp.where(taken[h] > 0.0, 0.0, NEG_BIG), NEG_BIG))
        extra = jnp.where(lane < SEL_LANES, sel_bias, qf_ref[h:h + 1, :] * own_coef)
        qa_ref[0, h, :, HEAD_DIM:2 * HEAD_DIM] = extra.astype(BF16)


def _moba_qprep(proj, k_mean, q_norm_w, q_feat, b, t, tq=256):
    nq = t // tq
    return pl.pallas_call(
        _moba_qprep_kernel,
        grid=(b, nq),
        in_specs=[
            pl.BlockSpec((tq, MOBA_W), lambda bi, ti: (bi * nq + ti, MOBA_OFF // MOBA_W)),
            pl.BlockSpec((1, HEAD_DIM, MOBA_W), lambda bi, ti: (bi, 0, 0)),
            pl.BlockSpec((1, HEAD_DIM), lambda bi, ti: (0, 0)),
            pl.BlockSpec((MOBA_HEADS, HEAD_DIM), lambda bi, ti: (0, 0)),
        ],
        out_specs=pl.BlockSpec((1, MOBA_HEADS, tq, 2 * HEAD_DIM), lambda bi, ti: (bi, 0, ti, 0)),
        out_shape=jax.ShapeDtypeStruct((b, MOBA_HEADS, t, 2 * HEAD_DIM), BF16),
        compiler_params=pltpu.CompilerParams(
            dimension_semantics=("parallel", "parallel"), vmem_limit_bytes=VMEM_LIMIT),
        name="moba_qprep",
    )(proj, k_mean, q_norm_w.reshape(1, HEAD_DIM), q_feat)


def _moba_attn_kernel(qa_ref, ka_ref, v_ref, z_ref, o_ref, m_ref, l_ref, acc_ref, *, group, hp):
    blk = MOBA_BLOCK
    d = HEAD_DIM
    gw = group * blk
    qi = pl.program_id(2)
    n_full = qi // group
    heads = range(hp)

    m_ref[...] = jnp.full(m_ref.shape, NEG_BIG, F32)
    l_ref[...] = jnp.zeros(l_ref.shape, F32)
    acc_ref[...] = jnp.zeros(acc_ref.shape, F32)

    def scores(off):
        return [_dot_nt(qa_ref[0, h], ka_ref[0, h, pl.ds(off, gw), :]) for h in heads]

    def update(s, off):
        m_prev = [m_ref[h] for h in heads]
        m_new = [jnp.maximum(m_prev[h], jnp.max(s[h], axis=-1, keepdims=True)) for h in heads]
        alpha = [jnp.exp(m_prev[h] - m_new[h]) for h in heads]
        p = [jnp.exp(s[h] - jnp.concatenate([m_new[h]] * (gw // d), axis=-1)) for h in heads]
        for h in heads:
            l_ref[h] = alpha[h] * l_ref[h] + jnp.sum(p[h], axis=-1, keepdims=True)
            m_ref[h] = m_new[h]
        pv = [_dot(p[h].astype(BF16), v_ref[pl.ds(off, gw), h * d:(h + 1) * d]) for h in heads]
        for h in heads:
            acc_ref[h] = alpha[h] * acc_ref[h] + pv[h]

    def body(g, carry):
        off = pl.multiple_of(g * gw, gw)
        update(scores(off), off)
        return carry

    lax.fori_loop(0, n_full, body, 0)

    off = pl.multiple_of(n_full * gw, gw)
    qpos = qi * blk + lax.broadcasted_iota(jnp.int32, (blk, gw), 0)
    kpos = off + lax.broadcasted_iota(jnp.int32, (blk, gw), 1)
    visible = kpos <= qpos
    update([jnp.where(visible, s, NEG_BIG) for s in scores(off)], off)

    for h in heads:
        sl = slice(h * d, (h + 1) * d)
        o = acc_ref[h] / l_ref[h]
        o_ref[:, sl] = (o * _silu(z_ref[:, sl].astype(F32))).astype(o_ref.dtype)


def _moba_attn(proj, q_aug, k_aug, b, t, group=4, hp=2):
    blk = MOBA_BLOCK
    nb = t // blk
    w = hp * HEAD_DIM
    vc = (MOBA_OFF + 2 * MOBA_W) // w
    zc = (MOBA_OFF + 3 * MOBA_W) // w
    assert nb % group == 0 and MOBA_HEADS % hp == 0 and MOBA_W % w == 0 and MOBA_OFF % w == 0
    return pl.pallas_call(
        functools.partial(_moba_attn_kernel, group=group, hp=hp),
        grid=(b, MOBA_HEADS // hp, nb),
        in_specs=[
            pl.BlockSpec((1, hp, blk, 2 * HEAD_DIM), lambda bi, hi, qi: (bi, hi, qi, 0)),
            pl.BlockSpec((1, hp, t, 2 * HEAD_DIM), lambda bi, hi, qi: (bi, hi, 0, 0)),
            pl.BlockSpec((t, w), lambda bi, hi, qi: (bi, vc + hi)),
            pl.BlockSpec((blk, w), lambda bi, hi, qi: (bi * nb + qi, zc + hi)),
        ],
        out_specs=pl.BlockSpec((blk, w), lambda bi, hi, qi: (bi * nb + qi, hi)),
        out_shape=jax.ShapeDtypeStruct((b * t, MOBA_W), BF16),
        scratch_shapes=[pltpu.VMEM((hp, blk, HEAD_DIM), F32),
                        pltpu.VMEM((hp, blk, HEAD_DIM), F32),
                        pltpu.VMEM((hp, blk, HEAD_DIM), F32)],
        compiler_params=pltpu.CompilerParams(
            dimension_semantics=("parallel", "parallel", "arbitrary"), vmem_limit_bytes=VMEM_LIMIT),
        name="moba_attn",
    )(q_aug, k_aug, proj, proj)


def _layer(x2d, b, t, norm_w, w_main, w_small, conv_w, a_log, dt_bias, ret_norm_w, gdn_norm_w,
           q_norm_w, k_norm_w, w_out, key_feat, q_feat):
    nb = t // MOBA_BLOCK
    proj, small = _in_proj(x2d, norm_w, w_main, w_small)
    y_ret = _retention(proj, ret_norm_w, b, t)
    y_gdn = _gdn(proj, small, conv_w, a_log, dt_bias, gdn_norm_w, b, t)
    k_aug, k_mean = _moba_prep(proj, k_norm_w, key_feat, b, t)
    k_mean = jnp.pad(k_mean.reshape(b, nb, MOBA_W), ((0, 0), (0, HEAD_DIM - nb), (0, 0)))
    q_aug = _moba_qprep(proj, k_mean, q_norm_w, q_feat, b, t)
    y_moba = _moba_attn(proj, q_aug, k_aug, b, t)
    return _out_proj(x2d, y_ret, y_gdn, y_moba, w_out)


def kernel(x, norm_w, w_in, conv_w, a_log, dt_bias, ret_norm_w, gdn_norm_w, q_norm_w, k_norm_w, w_out):
    b, t, d = x.shape
    depth = w_in.shape[0]
    assert t % MOBA_BLOCK == 0 and t // MOBA_BLOCK <= SEL_LANES
    gdn_lo = 4 * RET_W
    small_lo = gdn_lo + 4 * GDN_W
    moba_lo = small_lo + 2 * GDN_HEADS
    w_in_b = w_in.astype(BF16)
    w_main = jnp.concatenate(
        [w_in_b[:, :, gdn_lo:small_lo], w_in_b[:, :, moba_lo:], w_in_b[:, :, :gdn_lo]], axis=-1)
    w_small = jnp.pad(w_in_b[:, :, small_lo:moba_lo], ((0, 0), (0, 0), (0, SMALL_W - 2 * GDN_HEADS)))
    w_out_b = w_out.astype(BF16)
    key_feat = _moba_key_features(t)
    q_feat = _moba_query_features()
    x2d = x.reshape(b * t, d)
    for layer in range(depth):
        x2d = _layer(x2d, b, t, norm_w[layer], w_main[layer], w_small[layer], conv_w[layer],
                     a_log[layer], dt_bias[layer], ret_norm_w[layer], gdn_norm_w[layer],
                     q_norm_w[layer], k_norm_w[layer], w_out_b[layer], key_feat, q_feat)
    return x2d.reshape(b, t, d)
```

```python
import functools

import numpy as np
import jax
import jax.numpy as jnp
from jax import lax
from jax.experimental import pallas as pl
from jax.experimental.pallas import tpu as pltpu

F32 = jnp.float32
BF16 = jnp.bfloat16

HEAD_DIM = 128
RET_HEADS = 4
GDN_HEADS = 6
MOBA_HEADS = 6
RET_W = RET_HEADS * HEAD_DIM
GDN_W = GDN_HEADS * HEAD_DIM
MOBA_W = MOBA_HEADS * HEAD_DIM
GDN_CONV = 4
MOBA_BLOCK = 256
MOBA_TOP_K = 3
RMS_EPS = 1e-6
SCALE = HEAD_DIM ** -0.5

GDN_OFF = 0
MOBA_OFF = 4 * GDN_W
RET_OFF = 4 * GDN_W + 4 * MOBA_W
MAIN_W = RET_OFF + 4 * RET_W
SMALL_W = 128

RET_C = 256
GDN_C = 128
HALO = 8

NEG_BIG = -1e30
SEL_LANES = 32
VMEM_LIMIT = 56 * 1024 * 1024


def _dot(a, b):
    return jnp.dot(a, b, preferred_element_type=F32)


def _dot_nt(a, b):
    return lax.dot_general(a, b, (((1,), (1,)), ((), ())), preferred_element_type=F32)


def _dot_tn(a, b):
    return lax.dot_general(a, b, (((0,), (0,)), ((), ())), preferred_element_type=F32)


def _sigmoid(x):
    return 1.0 / (1.0 + jnp.exp(-x))


def _silu(x):
    return x * _sigmoid(x)


def _split3(x):
    x1 = x.astype(BF16)
    r = x - x1.astype(F32)
    x2 = r.astype(BF16)
    x3 = (r - x2.astype(F32)).astype(BF16)
    return x1, x2, x3


def _rmsnorm_rows(x, nw):
    ms = jnp.mean(x * x, axis=-1, keepdims=True)
    return (x * lax.rsqrt(ms + RMS_EPS) * nw).astype(BF16)


def _norm_kernel(x_ref, nw_ref, h_ref):
    h_ref[...] = _rmsnorm_rows(x_ref[...], nw_ref[...])


def _norm(x2d, norm_w, tm=512):
    m, d = x2d.shape
    return pl.pallas_call(
        _norm_kernel,
        grid=(m // tm,),
        in_specs=[pl.BlockSpec((tm, d), lambda i: (i, 0)), pl.BlockSpec((1, d), lambda i: (0, 0))],
        out_specs=pl.BlockSpec((tm, d), lambda i: (i, 0)),
        out_shape=jax.ShapeDtypeStruct((m, d), BF16),
        compiler_params=pltpu.CompilerParams(
            dimension_semantics=("parallel",), vmem_limit_bytes=VMEM_LIMIT),
        name="rmsnorm",
    )(x2d, norm_w.reshape(1, d))


def _in_proj_kernel(h_ref, wa_ref, wb_ref, ws_ref, o_ref, os_ref, *, na):
    j = pl.program_id(1)

    @pl.when(j == 0)
    def _():
        os_ref[...] = _dot(h_ref[...], ws_ref[0])

    @pl.when(j < na)
    def _():
        o_ref[...] = _dot(h_ref[...], wa_ref[0]).astype(o_ref.dtype)

    @pl.when(j >= na)
    def _():
        o_ref[...] = _dot(h_ref[...], wb_ref[0]).astype(o_ref.dtype)


def _in_proj(h2d, w_a, w_b, w_small, layer, tm=2048, tn=512):
    m, d = h2d.shape
    nr = 4 * RET_W // tn
    na = (4 * RET_W + 4 * GDN_W) // tn
    nb = 4 * MOBA_W // tn
    assert 4 * RET_W % tn == 0 and 4 * GDN_W % tn == 0 and 4 * MOBA_W % tn == 0 and m % tm == 0
    return pl.pallas_call(
        functools.partial(_in_proj_kernel, na=na),
        grid=(m // tm, na + nb),
        in_specs=[
            pl.BlockSpec((tm, d), lambda i, j: (i, 0)),
            pl.BlockSpec((1, d, tn), lambda i, j: (layer, 0, jnp.minimum(j, na - 1))),
            pl.BlockSpec((1, d, tn), lambda i, j: (layer, 0, jnp.maximum(j - na, 0))),
            pl.BlockSpec((1, d, SMALL_W), lambda i, j: (layer, 0, 0)),
        ],
        out_specs=[
            pl.BlockSpec((tm, tn), lambda i, j: (i, jnp.where(j < nr, j + (na - nr) + nb, j - nr))),
            pl.BlockSpec((tm, SMALL_W), lambda i, j: (i, 0)),
        ],
        out_shape=[
            jax.ShapeDtypeStruct((m, MAIN_W), BF16),
            jax.ShapeDtypeStruct((m, SMALL_W), F32),
        ],
        compiler_params=pltpu.CompilerParams(
            dimension_semantics=("parallel", "arbitrary"), vmem_limit_bytes=VMEM_LIMIT),
        name="in_proj",
    )(h2d, w_a, w_b, w_small)


def _out_proj_kernel(x_ref, yr_ref, yg_ref, ym_ref, w_ref, *rest, tn, with_norm):
    if with_norm:
        nw_ref, o_ref, h_ref = rest
    else:
        (o_ref,) = rest
    n = o_ref.shape[1]
    yr = yr_ref[...]
    yg = yg_ref[...]
    ym = ym_ref[...]
    for c in range(0, n, tn):
        acc = x_ref[:, c:c + tn]
        acc = acc + _dot(yr, w_ref[0, 0:RET_W, c:c + tn])
        acc = acc + _dot(yg, w_ref[0, RET_W:RET_W + GDN_W, c:c + tn])
        acc = acc + _dot(ym, w_ref[0, RET_W + GDN_W:RET_W + GDN_W + MOBA_W, c:c + tn])
        o_ref[:, c:c + tn] = acc
    if with_norm:
        h_ref[...] = _rmsnorm_rows(o_ref[...], nw_ref[...])


def _out_proj(x2d, y_ret, y_gdn, y_moba, w_out, layer, next_norm_w=None, tm=512, tn=512):
    m, d = x2d.shape
    with_norm = next_norm_w is not None
    in_specs = [
        pl.BlockSpec((tm, d), lambda i: (i, 0)),
        pl.BlockSpec((tm, RET_W), lambda i: (i, 0)),
        pl.BlockSpec((tm, GDN_W), lambda i: (i, 0)),
        pl.BlockSpec((tm, MOBA_W), lambda i: (i, 0)),
        pl.BlockSpec((1,) + w_out.shape[1:], lambda i: (layer, 0, 0)),
    ]
    args = [x2d, y_ret, y_gdn, y_moba, w_out]
    out_specs = [pl.BlockSpec((tm, d), lambda i: (i, 0))]
    out_shape = [jax.ShapeDtypeStruct((m, d), F32)]
    if with_norm:
        in_specs.append(pl.BlockSpec((1, d), lambda i: (0, 0)))
        args.append(next_norm_w.reshape(1, d))
        out_specs.append(pl.BlockSpec((tm, d), lambda i: (i, 0)))
        out_shape.append(jax.ShapeDtypeStruct((m, d), BF16))
    res = pl.pallas_call(
        functools.partial(_out_proj_kernel, tn=tn, with_norm=with_norm),
        grid=(m // tm,),
        in_specs=in_specs,
        out_specs=out_specs,
        out_shape=out_shape,
        compiler_params=pltpu.CompilerParams(
            dimension_semantics=("parallel",), vmem_limit_bytes=VMEM_LIMIT),
        name="out_proj",
    )(*args)
    return (res[0], res[1]) if with_norm else (res[0], None)


def _retention_consts(c):
    h = np.arange(RET_HEADS, dtype=np.float64)
    log_gamma = np.log1p(-np.exp2(-5.0 - h))
    pos = np.arange(c, dtype=np.float64)
    diff = pos[:, None] - pos[None, :]
    dmat = np.where(diff >= 0, np.exp(np.maximum(diff, 0.0) * log_gamma[:, None, None]), 0.0) * SCALE
    kws = np.exp((c - 1 - pos)[None, :] * log_gamma[:, None])
    qws = np.exp((pos + 1.0)[None, :] * log_gamma[:, None]) * SCALE
    kws = np.broadcast_to(kws[:, :, None], (RET_HEADS, c, HEAD_DIM))
    qws = np.broadcast_to(qws[:, :, None], (RET_HEADS, c, HEAD_DIM))
    chunk_decay = [float(v) for v in np.exp(c * log_gamma)]
    return (jnp.asarray(dmat, F32), jnp.asarray(kws, F32), jnp.asarray(qws, F32), chunk_decay)


def _retention_kernel(q_ref, k_ref, v_ref, g_ref, dm_ref, kws_ref, qws_ref, nw_ref, o_ref, st_ref,
                      *, chunk_decay):
    @pl.when(pl.program_id(1) == 0)
    def _():
        st_ref[...] = jnp.zeros_like(st_ref)

    for h in range(RET_HEADS):
        sl = slice(h * HEAD_DIM, (h + 1) * HEAD_DIM)
        q = q_ref[:, sl]
        k = k_ref[:, sl]
        v = v_ref[:, sl]
        s = _dot_nt(q, k) * dm_ref[h]
        intra = _dot(s.astype(BF16), v)
        state = st_ref[h]
        qw = (q.astype(F32) * qws_ref[h]).astype(BF16)
        o = intra + _dot(qw, state.astype(BF16))
        kw = (k.astype(F32) * kws_ref[h]).astype(BF16)
        st_ref[h] = state * chunk_decay[h] + _dot_tn(kw, v)
        ms = jnp.mean(o * o, axis=-1, keepdims=True)
        y = o * lax.rsqrt(ms + RMS_EPS) * nw_ref[h:h + 1, :]
        o_ref[:, sl] = (y * _silu(g_ref[:, sl].astype(F32))).astype(o_ref.dtype)


def _retention(proj, ret_norm_w, b, t):
    c = RET_C
    nt = t // c
    dmat, kws, qws, chunk_decay = _retention_consts(c)
    base = RET_OFF // RET_W

    def col(ci):
        return pl.BlockSpec((c, RET_W), lambda bi, ti: (bi * nt + ti, base + ci))

    def const3(shape):
        return pl.BlockSpec(shape, lambda bi, ti: (0, 0, 0))

    return pl.pallas_call(
        functools.partial(_retention_kernel, chunk_decay=chunk_decay),
        grid=(b, nt),
        in_specs=[col(0), col(1), col(2), col(3),
                  const3(dmat.shape), const3(kws.shape), const3(qws.shape),
                  pl.BlockSpec((RET_HEADS, HEAD_DIM), lambda bi, ti: (0, 0))],
        out_specs=pl.BlockSpec((c, RET_W), lambda bi, ti: (bi * nt + ti, 0)),
        out_shape=jax.ShapeDtypeStruct((b * t, RET_W), BF16),
        scratch_shapes=[pltpu.VMEM((RET_HEADS, HEAD_DIM, HEAD_DIM), F32)],
        compiler_params=pltpu.CompilerParams(
            dimension_semantics=("parallel", "arbitrary"), vmem_limit_bytes=VMEM_LIMIT),
        name="retention",
    )(proj, proj, proj, proj, dmat, kws, qws, ret_norm_w)


def _gdn_kernel(qkv_ref, z_ref, sm_ref, cw_ref, al_ref, dtb_ref, nw_ref, tril_ref, o_ref,
                buf_ref, st_ref, *, nsub):
    c = GDN_C
    d = HEAD_DIM
    tt = nsub * c
    heads = range(GDN_HEADS)

    @pl.when(pl.program_id(1) == 0)
    def _():
        buf_ref[0:HALO, :] = jnp.zeros((HALO, buf_ref.shape[1]), F32)
        st_ref[...] = jnp.zeros_like(st_ref)

    buf_ref[HALO:HALO + tt, :] = qkv_ref[...].astype(F32)

    def conv_act(s, col):
        sl = slice(col * d, (col + 1) * d)
        r0 = HALO + s * c
        acc = cw_ref[GDN_CONV - 1:GDN_CONV, sl] * buf_ref[r0:r0 + c, sl]
        for j in range(GDN_CONV - 1):
            sh = GDN_CONV - 1 - j
            acc = acc + cw_ref[j:j + 1, sl] * buf_ref[r0 - sh:r0 - sh + c, sl]
        return _silu(acc)

    def l2n(t):
        return t * lax.rsqrt(jnp.sum(t * t, axis=-1, keepdims=True) + RMS_EPS)

    sm = sm_ref[...]
    x = sm + dtb_ref[...]
    softplus = jnp.maximum(x, 0.0) + jnp.log1p(jnp.exp(-jnp.abs(x)))
    g = -jnp.exp(al_ref[...]) * softplus
    beta = _sigmoid(sm)
    tril = tril_ref[...]
    gc, gct = [], []
    for s in range(nsub):
        g1, g2, g3 = _split3(g[s * c:(s + 1) * c])
        gcs = _dot(tril, g1) + _dot(tril, g2) + _dot(tril, g3)
        gc.append(gcs)
        gct.append(gcs.T)

    row = lax.broadcasted_iota(jnp.int32, (c, c), 0)
    coli = lax.broadcasted_iota(jnp.int32, (c, c), 1)
    causal = row >= coli
    strict = row > coli
    eye = jnp.where(row == coli, 1.0, 0.0).astype(F32)
    diag8 = strict & ((row >> 3) == (coli >> 3))
    merge_masks = [((row >> (lv + 1)) == (coli >> (lv + 1))) & ((row >> lv) != (coli >> lv)) & strict
                   for lv in range(3, 7)]

    units = [(s, h) for s in range(nsub) for h in heads]
    un = range(len(units))
    q = [l2n(conv_act(s, h)) * SCALE for s, h in units]
    k = [l2n(conv_act(s, GDN_HEADS + h)) for s, h in units]
    v = [conv_act(s, 2 * GDN_HEADS + h) for s, h in units]
    gcol = [jnp.broadcast_to(gc[s][:, h:h + 1], (c, c)) for s, h in units]
    dm = [jnp.exp(jnp.where(causal, gcol[u] - gct[s][h:h + 1, :], NEG_BIG))
          for u, (s, h) in enumerate(units)]
    bb = [jnp.broadcast_to(beta[s * c:(s + 1) * c, GDN_HEADS + h:GDN_HEADS + h + 1], (c, d))
          for s, h in units]
    eg = [jnp.exp(gcol[u]) for u in un]
    glast = [gcol[u][c - 1:c, :] for u in un]
    ek = [jnp.exp(glast[u] - gcol[u]) for u in un]
    kb = [k[u] * bb[u] for u in un]
    k_b = [k[u].astype(BF16) for u in un]

    sc = [_dot_nt(jnp.concatenate([kb[u], q[u]], axis=0).astype(BF16), k_b[u]) for u in un]
    mm = [jnp.where(strict, sc[u][:c] * dm[u], 0.0) for u in un]
    qk_b = [(sc[u][c:] * dm[u]).astype(BF16) for u in un]

    md = [jnp.where(diag8, mm[u], 0.0) for u in un]
    md_b = [md[u].astype(BF16) for u in un]
    x0 = [eye - md[u] for u in un]
    md2_b = [_dot(md_b[u], md_b[u]).astype(BF16) for u in un]
    t1 = [_dot(md2_b[u], jnp.concatenate([x0[u].astype(BF16), md2_b[u]], axis=-1)) for u in un]
    x1 = [x0[u] + t1[u][:, :c] for u in un]
    md4_b = [t1[u][:, c:].astype(BF16) for u in un]
    xi = [x1[u] + _dot(md4_b[u], x1[u].astype(BF16)) for u in un]
    for mask in merge_masks:
        mo_b = [jnp.where(mask, mm[u], 0.0).astype(BF16) for u in un]
        xi_b = [xi[u].astype(BF16) for u in un]
        y_b = [_dot(xi_b[u], mo_b[u]).astype(BF16) for u in un]
        xi = [xi[u] - _dot(y_b[u], xi_b[u]) for u in un]
    xc_b = [(xi[u] - eye).astype(BF16) for u in un]

    rhs = [jnp.concatenate([kb[u] * eg[u], v[u] * bb[u]], axis=-1) for u in un]
    wu = [rhs[u] + _dot(xc_b[u], rhs[u].astype(BF16)) for u in un]
    wq_b = [jnp.concatenate([wu[u][:, :d], q[u] * eg[u]], axis=0).astype(BF16) for u in un]
    kg_b = [(k[u] * ek[u]).astype(BF16) for u in un]
    elast = [jnp.exp(glast[u]) for u in un]

    state = [st_ref[h] for h in heads]
    for s in range(nsub):
        us = [s * GDN_HEADS + h for h in heads]
        state_b = [state[h].astype(BF16) for h in heads]
        ws = [_dot(wq_b[us[h]], state_b[h]) for h in heads]
        v_new_b = [(wu[us[h]][:, d:] - ws[h][:c]).astype(BF16) for h in heads]
        o = [ws[h][c:] + _dot(qk_b[us[h]], v_new_b[h]) for h in heads]
        state = [state[h] * elast[us[h]] + _dot_tn(kg_b[us[h]], v_new_b[h]) for h in heads]
        for h in heads:
            sl = slice(h * d, (h + 1) * d)
            rows = slice(s * c, (s + 1) * c)
            ms = jnp.mean(o[h] * o[h], axis=-1, keepdims=True)
            y = o[h] * lax.rsqrt(ms + RMS_EPS) * nw_ref[...]
            o_ref[rows, sl] = (y * _silu(z_ref[rows, sl].astype(F32))).astype(o_ref.dtype)
    for h in heads:
        st_ref[h] = state[h]

    buf_ref[0:HALO, :] = buf_ref[tt:tt + HALO, :]


def _gdn(proj, small, conv_w, a_log, dt_bias, gdn_norm_w, b, t, nsub=2):
    c = GDN_C
    tt = nsub * c
    nt = t // tt
    qkv_w = 3 * GDN_W
    pad = SMALL_W - GDN_HEADS
    al_row = jnp.pad(a_log.astype(F32), (0, pad)).reshape(1, SMALL_W)
    dtb_row = jnp.pad(dt_bias.astype(F32), (0, pad)).reshape(1, SMALL_W)
    tril = jnp.asarray(np.tril(np.ones((c, c), np.float32)), BF16)
    return pl.pallas_call(
        functools.partial(_gdn_kernel, nsub=nsub),
        grid=(b, nt),
        in_specs=[
            pl.BlockSpec((tt, qkv_w), lambda bi, ti: (bi * nt + ti, GDN_OFF // qkv_w)),
            pl.BlockSpec((tt, GDN_W), lambda bi, ti: (bi * nt + ti, (GDN_OFF + qkv_w) // GDN_W)),
            pl.BlockSpec((tt, SMALL_W), lambda bi, ti: (bi * nt + ti, 0)),
            pl.BlockSpec((GDN_CONV, qkv_w), lambda bi, ti: (0, 0)),
            pl.BlockSpec((1, SMALL_W), lambda bi, ti: (0, 0)),
            pl.BlockSpec((1, SMALL_W), lambda bi, ti: (0, 0)),
            pl.BlockSpec((1, HEAD_DIM), lambda bi, ti: (0, 0)),
            pl.BlockSpec((c, c), lambda bi, ti: (0, 0)),
        ],
        out_specs=pl.BlockSpec((tt, GDN_W), lambda bi, ti: (bi * nt + ti, 0)),
        out_shape=jax.ShapeDtypeStruct((b * t, GDN_W), BF16),
        scratch_shapes=[pltpu.VMEM((HALO + tt, qkv_w), F32),
                        pltpu.VMEM((GDN_HEADS, HEAD_DIM, HEAD_DIM), F32)],
        compiler_params=pltpu.CompilerParams(
            dimension_semantics=("parallel", "arbitrary"), vmem_limit_bytes=VMEM_LIMIT),
        name="gdn",
    )(proj, proj, small, conv_w, al_row, dtb_row, gdn_norm_w.reshape(1, HEAD_DIM), tril)


def _moba_key_features(t):
    pos = np.arange(t)
    blk = pos // MOBA_BLOCK
    off = pos % MOBA_BLOCK
    f = np.zeros((t, HEAD_DIM), np.float32)
    f[pos, blk] = 1.0
    f[:, SEL_LANES] = 1.0
    for p in range(3):
        f[:, SEL_LANES + 1 + p] = blk
        f[:, SEL_LANES + 4 + p] = off // 16
        f[:, SEL_LANES + 7 + p] = off % 16
    return jnp.asarray(f, BF16)


def _moba_query_features():
    hh = MOBA_HEADS
    slopes = np.exp2(-8.0 * np.arange(1, hh + 1, dtype=np.float64) / hh).astype(np.float32)
    s = jnp.asarray(slopes, F32)
    s1 = s.astype(BF16).astype(F32)
    s2 = (s - s1).astype(BF16).astype(F32)
    s3 = (s - s1 - s2).astype(BF16).astype(F32)
    f = jnp.zeros((hh, HEAD_DIM), F32)
    f = f.at[:, SEL_LANES].set(-s * MOBA_BLOCK)
    for p, sp in enumerate((s1, s2, s3)):
        f = f.at[:, SEL_LANES + 1 + p].set(sp * MOBA_BLOCK)
        f = f.at[:, SEL_LANES + 4 + p].set(sp * 16.0)
        f = f.at[:, SEL_LANES + 7 + p].set(sp)
    return f


def _moba_prep_kernel(k_ref, v_ref, nw_ref, pf_ref, ka_ref, va_ref, km_ref):
    ones = jnp.ones((k_ref.shape[0], HEAD_DIM), BF16)
    for h in range(MOBA_HEADS):
        sl = slice(h * HEAD_DIM, (h + 1) * HEAD_DIM)
        k = k_ref[:, sl].astype(F32)
        ms = jnp.mean(k * k, axis=-1, keepdims=True)
        kn = k * lax.rsqrt(ms + RMS_EPS) * nw_ref[...]
        ka_ref[0, h, :, 0:HEAD_DIM] = kn.astype(BF16)
        ka_ref[0, h, :, HEAD_DIM:2 * HEAD_DIM] = pf_ref[...]
        km_ref[0, 0, :, sl] = jnp.mean(kn, axis=0, keepdims=True)
        va_ref[0, h, :, 0:HEAD_DIM] = v_ref[:, sl]
        va_ref[0, h, :, HEAD_DIM:2 * HEAD_DIM] = ones


def _moba_prep(proj, k_norm_w, key_feat, b, t):
    blk = MOBA_BLOCK
    nb = t // blk
    kc = (MOBA_OFF + MOBA_W) // MOBA_W
    aug = pl.BlockSpec((1, MOBA_HEADS, blk, 2 * HEAD_DIM), lambda bi, ti: (bi, 0, ti, 0))
    aug_shape = jax.ShapeDtypeStruct((b, MOBA_HEADS, t, 2 * HEAD_DIM), BF16)
    return pl.pallas_call(
        _moba_prep_kernel,
        grid=(b, nb),
        in_specs=[
            pl.BlockSpec((blk, MOBA_W), lambda bi, ti: (bi * nb + ti, kc)),
            pl.BlockSpec((blk, MOBA_W), lambda bi, ti: (bi * nb + ti, kc + 1)),
            pl.BlockSpec((1, HEAD_DIM), lambda bi, ti: (0, 0)),
            pl.BlockSpec((blk, HEAD_DIM), lambda bi, ti: (ti, 0)),
        ],
        out_specs=[aug, aug, pl.BlockSpec((1, 1, 1, MOBA_W), lambda bi, ti: (bi, ti, 0, 0))],
        out_shape=[aug_shape, aug_shape, jax.ShapeDtypeStruct((b, nb, 1, MOBA_W), F32)],
        compiler_params=pltpu.CompilerParams(
            dimension_semantics=("parallel", "parallel"), vmem_limit_bytes=VMEM_LIMIT),
        name="moba_prep",
    )(proj, proj, k_norm_w.reshape(1, HEAD_DIM), key_feat)


def _moba_qprep_kernel(q_ref, km_ref, qnw_ref, qf_ref, qa_ref):
    tq = q_ref.shape[0]
    lane = lax.broadcasted_iota(jnp.int32, (tq, HEAD_DIM), 1)
    lane_f = lane.astype(F32)
    rowi = lax.broadcasted_iota(jnp.int32, (tq, HEAD_DIM), 0)
    own = (pl.program_id(1) * tq + rowi) // MOBA_BLOCK
    past = lane < own
    own_coef = jnp.where(lane == SEL_LANES, own.astype(F32), 1.0)
    heads = range(MOBA_HEADS)
    hsl = [slice(h * HEAD_DIM, (h + 1) * HEAD_DIM) for h in heads]

    def qnorm(h):
        q = q_ref[:, hsl[h]].astype(F32)
        ms = jnp.mean(q * q, axis=-1, keepdims=True)
        return q * lax.rsqrt(ms + RMS_EPS) * qnw_ref[...]

    qn = [qnorm(h) for h in heads]
    for h in heads:
        qa_ref[0, h, :, 0:HEAD_DIM] = (qn[h] * SCALE).astype(BF16)

    def gate_scores(h):
        km = km_ref[0, :, hsl[h]]
        q1 = qn[h].astype(BF16)
        q2 = (qn[h] - q1.astype(F32)).astype(BF16)
        k1 = km.astype(BF16)
        k2 = (km - k1.astype(F32)).astype(BF16)
        return jnp.where(past, _dot_nt(q1, k1) + _dot_nt(q1, k2) + _dot_nt(q2, k1), -jnp.inf)

    gate = [gate_scores(h) for h in heads]
    taken = [jnp.zeros((tq, HEAD_DIM), F32) for _ in heads]
    for _ in range(MOBA_TOP_K):
        gm = [jnp.where(taken[h] > 0.0, -jnp.inf, gate[h]) for h in heads]
        mx = [jnp.max(gm[h], axis=-1, keepdims=True) for h in heads]
        cand = [jnp.where(taken[h] > 0.0, float(HEAD_DIM),
                          jnp.where(gm[h] == mx[h], lane_f, float(HEAD_DIM))) for h in heads]
        idx = [jnp.min(cand[h], axis=-1, keepdims=True) for h in heads]
        taken = [jnp.where(lane_f == idx[h], 1.0, taken[h]) for h in heads]
    for h in heads:
        sel_bias = jnp.where(lane == own, 0.0,
                             jnp.where(past, jnp.where(taken[h] > 0.0, 0.0, NEG_BIG), NEG_BIG))
        extra = jnp.where(lane < SEL_LANES, sel_bias, qf_ref[h:h + 1, :] * own_coef)
        qa_ref[0, h, :, HEAD_DIM:2 * HEAD_DIM] = extra.astype(BF16)


def _moba_qprep(proj, k_mean, q_norm_w, q_feat, b, t, tq=256):
    nq = t // tq
    return pl.pallas_call(
        _moba_qprep_kernel,
        grid=(b, nq),
        in_specs=[
            pl.BlockSpec((tq, MOBA_W), lambda bi, ti: (bi * nq + ti, MOBA_OFF // MOBA_W)),
            pl.BlockSpec((1, HEAD_DIM, MOBA_W), lambda bi, ti: (bi, 0, 0)),
            pl.BlockSpec((1, HEAD_DIM), lambda bi, ti: (0, 0)),
            pl.BlockSpec((MOBA_HEADS, HEAD_DIM), lambda bi, ti: (0, 0)),
        ],
        out_specs=pl.BlockSpec((1, MOBA_HEADS, tq, 2 * HEAD_DIM), lambda bi, ti: (bi, 0, ti, 0)),
        out_shape=jax.ShapeDtypeStruct((b, MOBA_HEADS, t, 2 * HEAD_DIM), BF16),
        compiler_params=pltpu.CompilerParams(
            dimension_semantics=("parallel", "parallel"), vmem_limit_bytes=VMEM_LIMIT),
        name="moba_qprep",
    )(proj, k_mean, q_norm_w.reshape(1, HEAD_DIM), q_feat)


def _moba_attn_kernel(qa_ref, ka_ref, va_ref, z_ref, o_ref, s_ref, m_ref, acc_ref, *, group, hp):
    blk = MOBA_BLOCK
    d = HEAD_DIM
    gw = group * blk
    qi = pl.program_id(2)
    n_full = qi // group
    heads = range(hp)

    m_ref[...] = jnp.full(m_ref.shape, NEG_BIG, F32)
    acc_ref[...] = jnp.zeros(acc_ref.shape, F32)

    def store_scores(slot, g):
        off = pl.multiple_of(g * gw, gw)
        for h in heads:
            s_ref[slot, h] = _dot_nt(qa_ref[0, h], ka_ref[0, h, pl.ds(off, gw), :])

    def update(s, g):
        off = pl.multiple_of(g * gw, gw)
        m_prev = [m_ref[h] for h in heads]
        m_new = [jnp.maximum(m_prev[h], jnp.max(s[h], axis=-1, keepdims=True)) for h in heads]
        alpha = [jnp.exp(m_prev[h] - m_new[h]) for h in heads]
        p = [jnp.exp(s[h] - jnp.concatenate([m_new[h]] * (gw // d), axis=-1)).astype(BF16)
             for h in heads]
        for h in heads:
            m_ref[h] = m_new[h]
        pv = [_dot(p[h], va_ref[0, h, pl.ds(off, gw), :]) for h in heads]
        for h in heads:
            acc_ref[h] = jnp.concatenate([alpha[h], alpha[h]], axis=-1) * acc_ref[h] + pv[h]

    store_scores(0, 0)

    def body(g, carry):
        slot = g % 2
        s = [s_ref[slot, h] for h in heads]
        update(s, g)
        store_scores(1 - slot, g + 1)
        return carry

    lax.fori_loop(0, n_full, body, 0)

    qpos = qi * blk + lax.broadcasted_iota(jnp.int32, (blk, gw), 0)
    kpos = n_full * gw + lax.broadcasted_iota(jnp.int32, (blk, gw), 1)
    visible = kpos <= qpos
    slot = n_full % 2
    update([jnp.where(visible, s_ref[slot, h], NEG_BIG) for h in heads], n_full)

    for h in heads:
        sl = slice(h * d, (h + 1) * d)
        o = acc_ref[h, :, 0:d] / acc_ref[h, :, d:2 * d]
        o_ref[:, sl] = (o * _silu(z_ref[:, sl].astype(F32))).astype(o_ref.dtype)


def _moba_attn(proj, q_aug, k_aug, v_aug, b, t, group=4, hp=2):
    blk = MOBA_BLOCK
    nb = t // blk
    w = hp * HEAD_DIM
    zc = (MOBA_OFF + 3 * MOBA_W) // w
    assert nb % group == 0 and MOBA_HEADS % hp == 0 and MOBA_W % w == 0 and MOBA_OFF % w == 0
    resident = pl.BlockSpec((1, hp, t, 2 * HEAD_DIM), lambda bi, hi, qi: (bi, hi, 0, 0))
    return pl.pallas_call(
        functools.partial(_moba_attn_kernel, group=group, hp=hp),
        grid=(b, MOBA_HEADS // hp, nb),
        in_specs=[
            pl.BlockSpec((1, hp, blk, 2 * HEAD_DIM), lambda bi, hi, qi: (bi, hi, qi, 0)),
            resident,
            resident,
            pl.BlockSpec((blk, w), lambda bi, hi, qi: (bi * nb + qi, zc + hi)),
        ],
        out_specs=pl.BlockSpec((blk, w), lambda bi, hi, qi: (bi * nb + qi, hi)),
        out_shape=jax.ShapeDtypeStruct((b * t, MOBA_W), BF16),
        scratch_shapes=[pltpu.VMEM((2, hp, blk, group * blk), F32),
                        pltpu.VMEM((hp, blk, HEAD_DIM), F32),
                        pltpu.VMEM((hp, blk, 2 * HEAD_DIM), F32)],
        compiler_params=pltpu.CompilerParams(
            dimension_semantics=("parallel", "parallel", "arbitrary"), vmem_limit_bytes=VMEM_LIMIT),
        name="moba_attn",
    )(q_aug, k_aug, v_aug, proj)


def _layer(x2d, h2d, b, t, layer, w_a, w_b, w_small, conv_w, a_log, dt_bias, ret_norm_w, gdn_norm_w,
           q_norm_w, k_norm_w, w_out, next_norm_w, key_feat, q_feat):
    nb = t // MOBA_BLOCK
    proj, small = _in_proj(h2d, w_a, w_b, w_small, layer)
    y_ret = _retention(proj, ret_norm_w, b, t)
    y_gdn = _gdn(proj, small, conv_w, a_log, dt_bias, gdn_norm_w, b, t)
    k_aug, v_aug, k_mean = _moba_prep(proj, k_norm_w, key_feat, b, t)
    k_mean = jnp.pad(k_mean.reshape(b, nb, MOBA_W), ((0, 0), (0, HEAD_DIM - nb), (0, 0)))
    q_aug = _moba_qprep(proj, k_mean, q_norm_w, q_feat, b, t)
    y_moba = _moba_attn(proj, q_aug, k_aug, v_aug, b, t)
    return _out_proj(x2d, y_ret, y_gdn, y_moba, w_out, layer, next_norm_w)


def kernel(x, norm_w, w_in, conv_w, a_log, dt_bias, ret_norm_w, gdn_norm_w, q_norm_w, k_norm_w, w_out):
    b, t, d = x.shape
    depth = w_in.shape[0]
    assert t % MOBA_BLOCK == 0 and t // MOBA_BLOCK <= SEL_LANES
    small_lo = 4 * RET_W + 4 * GDN_W
    moba_lo = small_lo + 2 * GDN_HEADS
    w_a = w_in.astype(BF16)
    w_b = w_a[:, :, moba_lo:]
    w_small = jnp.pad(w_a[:, :, small_lo:moba_lo], ((0, 0), (0, 0), (0, SMALL_W - 2 * GDN_HEADS)))
    w_out_b = w_out.astype(BF16)
    key_feat = _moba_key_features(t)
    q_feat = _moba_query_features()
    x2d = x.reshape(b * t, d)
    h2d = _norm(x2d, norm_w[0])
    for layer in range(depth):
        next_norm_w = norm_w[layer + 1] if layer + 1 < depth else None
        x2d, h2d = _layer(x2d, h2d, b, t, layer, w_a, w_b, w_small, conv_w[layer], a_log[layer],
                          dt_bias[layer], ret_norm_w[layer], gdn_norm_w[layer], q_norm_w[layer],
                          k_norm_w[layer], w_out_b, next_norm_w, key_feat, q_feat)
    return x2d.reshape(b, t, d)
```

```python
import functools

import numpy as np
import jax
import jax.numpy as jnp
from jax import lax
from jax.experimental import pallas as pl
from jax.experimental.pallas import tpu as pltpu

F32 = jnp.float32
BF16 = jnp.bfloat16

HEAD_DIM = 128
RET_HEADS = 4
GDN_HEADS = 6
MOBA_HEADS = 6
RET_W = RET_HEADS * HEAD_DIM
GDN_W = GDN_HEADS * HEAD_DIM
MOBA_W = MOBA_HEADS * HEAD_DIM
GDN_CONV = 4
MOBA_BLOCK = 256
MOBA_TOP_K = 3
RMS_EPS = 1e-6
SCALE = HEAD_DIM ** -0.5

GDN_OFF = 0
MOBA_OFF = 4 * GDN_W
RET_OFF = 4 * GDN_W + 4 * MOBA_W
MAIN_W = RET_OFF + 4 * RET_W
SMALL_W = 128

RET_C = 256
GDN_C = 128
HALO = 8

NEG_BIG = -1e30
SEL_LANES = 32
VMEM_LIMIT = 56 * 1024 * 1024


def _dot(a, b):
    return jnp.dot(a, b, preferred_element_type=F32)


def _dot_nt(a, b):
    return lax.dot_general(a, b, (((1,), (1,)), ((), ())), preferred_element_type=F32)


def _dot_tn(a, b):
    return lax.dot_general(a, b, (((0,), (0,)), ((), ())), preferred_element_type=F32)


def _sigmoid(x):
    return 1.0 / (1.0 + jnp.exp(-x))


def _silu(x):
    return x * _sigmoid(x)


def _split3(x):
    x1 = x.astype(BF16)
    r = x - x1.astype(F32)
    x2 = r.astype(BF16)
    x3 = (r - x2.astype(F32)).astype(BF16)
    return x1, x2, x3


def _rmsnorm_rows(x, nw):
    ms = jnp.mean(x * x, axis=-1, keepdims=True)
    return (x * lax.rsqrt(ms + RMS_EPS) * nw).astype(BF16)


def _norm_kernel(x_ref, nw_ref, h_ref):
    h_ref[...] = _rmsnorm_rows(x_ref[...], nw_ref[...])


def _norm(x2d, norm_w, tm=512):
    m, d = x2d.shape
    return pl.pallas_call(
        _norm_kernel,
        grid=(m // tm,),
        in_specs=[pl.BlockSpec((tm, d), lambda i: (i, 0)), pl.BlockSpec((1, d), lambda i: (0, 0))],
        out_specs=pl.BlockSpec((tm, d), lambda i: (i, 0)),
        out_shape=jax.ShapeDtypeStruct((m, d), BF16),
        compiler_params=pltpu.CompilerParams(
            dimension_semantics=("parallel",), vmem_limit_bytes=VMEM_LIMIT),
        name="rmsnorm",
    )(x2d, norm_w.reshape(1, d))


def _in_proj_kernel(h_ref, wa_ref, wb_ref, ws_ref, o_ref, os_ref, *, na):
    j = pl.program_id(1)

    @pl.when(j == 0)
    def _():
        os_ref[...] = _dot(h_ref[...], ws_ref[0].astype(BF16))

    @pl.when(j < na)
    def _():
        o_ref[...] = _dot(h_ref[...], wa_ref[0].astype(BF16)).astype(o_ref.dtype)

    @pl.when(j >= na)
    def _():
        o_ref[...] = _dot(h_ref[...], wb_ref[0].astype(BF16)).astype(o_ref.dtype)


def _in_proj(h2d, w_a, w_b, w_small, layer, tm=2048, tn=512):
    m, d = h2d.shape
    nr = 4 * RET_W // tn
    na = (4 * RET_W + 4 * GDN_W) // tn
    nb = 4 * MOBA_W // tn
    assert 4 * RET_W % tn == 0 and 4 * GDN_W % tn == 0 and 4 * MOBA_W % tn == 0 and m % tm == 0
    return pl.pallas_call(
        functools.partial(_in_proj_kernel, na=na),
        grid=(m // tm, na + nb),
        in_specs=[
            pl.BlockSpec((tm, d), lambda i, j: (i, 0)),
            pl.BlockSpec((1, d, tn), lambda i, j: (layer, 0, jnp.minimum(j, na - 1))),
            pl.BlockSpec((1, d, tn), lambda i, j: (layer, 0, jnp.maximum(j - na, 0))),
            pl.BlockSpec((1, d, SMALL_W), lambda i, j: (layer, 0, 0)),
        ],
        out_specs=[
            pl.BlockSpec((tm, tn), lambda i, j: (i, jnp.where(j < nr, j + (na - nr) + nb, j - nr))),
            pl.BlockSpec((tm, SMALL_W), lambda i, j: (i, 0)),
        ],
        out_shape=[
            jax.ShapeDtypeStruct((m, MAIN_W), BF16),
            jax.ShapeDtypeStruct((m, SMALL_W), F32),
        ],
        compiler_params=pltpu.CompilerParams(
            dimension_semantics=("parallel", "arbitrary"), vmem_limit_bytes=VMEM_LIMIT),
        name="in_proj",
    )(h2d, w_a, w_b, w_small)


def _out_proj_kernel(x_ref, yr_ref, yg_ref, ym_ref, w_ref, *rest, tn, with_norm):
    if with_norm:
        nw_ref, o_ref, h_ref = rest
    else:
        (o_ref,) = rest
    n = o_ref.shape[1]
    yr = yr_ref[...]
    yg = yg_ref[...]
    ym = ym_ref[...]
    for c in range(0, n, tn):
        acc = x_ref[:, c:c + tn]
        acc = acc + _dot(yr, w_ref[0, 0:RET_W, c:c + tn])
        acc = acc + _dot(yg, w_ref[0, RET_W:RET_W + GDN_W, c:c + tn])
        acc = acc + _dot(ym, w_ref[0, RET_W + GDN_W:RET_W + GDN_W + MOBA_W, c:c + tn])
        o_ref[:, c:c + tn] = acc
    if with_norm:
        h_ref[...] = _rmsnorm_rows(o_ref[...], nw_ref[...])


def _out_proj(x2d, y_ret, y_gdn, y_moba, w_out, layer, next_norm_w=None, tm=512, tn=512):
    m, d = x2d.shape
    with_norm = next_norm_w is not None
    in_specs = [
        pl.BlockSpec((tm, d), lambda i: (i, 0)),
        pl.BlockSpec((tm, RET_W), lambda i: (i, 0)),
        pl.BlockSpec((tm, GDN_W), lambda i: (i, 0)),
        pl.BlockSpec((tm, MOBA_W), lambda i: (i, 0)),
        pl.BlockSpec((1,) + w_out.shape[1:], lambda i: (layer, 0, 0)),
    ]
    args = [x2d, y_ret, y_gdn, y_moba, w_out]
    out_specs = [pl.BlockSpec((tm, d), lambda i: (i, 0))]
    out_shape = [jax.ShapeDtypeStruct((m, d), F32)]
    if with_norm:
        in_specs.append(pl.BlockSpec((1, d), lambda i: (0, 0)))
        args.append(next_norm_w.reshape(1, d))
        out_specs.append(pl.BlockSpec((tm, d), lambda i: (i, 0)))
        out_shape.append(jax.ShapeDtypeStruct((m, d), BF16))
    res = pl.pallas_call(
        functools.partial(_out_proj_kernel, tn=tn, with_norm=with_norm),
        grid=(m // tm,),
        in_specs=in_specs,
        out_specs=out_specs,
        out_shape=out_shape,
        compiler_params=pltpu.CompilerParams(
            dimension_semantics=("parallel",), vmem_limit_bytes=VMEM_LIMIT),
        name="out_proj",
    )(*args)
    return (res[0], res[1]) if with_norm else (res[0], None)


def _retention_consts(c):
    h = np.arange(RET_HEADS, dtype=np.float64)
    log_gamma = np.log1p(-np.exp2(-5.0 - h))
    pos = np.arange(c, dtype=np.float64)
    diff = pos[:, None] - pos[None, :]
    dmat = np.where(diff >= 0, np.exp(np.maximum(diff, 0.0) * log_gamma[:, None, None]), 0.0) * SCALE
    kws = np.exp((c - 1 - pos)[None, :] * log_gamma[:, None])
    qws = np.exp((pos + 1.0)[None, :] * log_gamma[:, None]) * SCALE
    kws = np.broadcast_to(kws[:, :, None], (RET_HEADS, c, HEAD_DIM))
    qws = np.broadcast_to(qws[:, :, None], (RET_HEADS, c, HEAD_DIM))
    chunk_decay = [float(v) for v in np.exp(c * log_gamma)]
    return (jnp.asarray(dmat, F32), jnp.asarray(kws, F32), jnp.asarray(qws, F32), chunk_decay)


def _retention_kernel(q_ref, k_ref, v_ref, g_ref, dm_ref, kws_ref, qws_ref, nw_ref, o_ref, st_ref,
                      *, chunk_decay):
    @pl.when(pl.program_id(1) == 0)
    def _():
        st_ref[...] = jnp.zeros_like(st_ref)

    heads = range(RET_HEADS)
    hsl = [slice(h * HEAD_DIM, (h + 1) * HEAD_DIM) for h in heads]
    q = [q_ref[:, hsl[h]] for h in heads]
    k = [k_ref[:, hsl[h]] for h in heads]
    v = [v_ref[:, hsl[h]] for h in heads]
    s_b = [(_dot_nt(q[h], k[h]) * dm_ref[h]).astype(BF16) for h in heads]
    state = [st_ref[h] for h in heads]
    qw = [(q[h].astype(F32) * qws_ref[h]).astype(BF16) for h in heads]
    kw = [(k[h].astype(F32) * kws_ref[h]).astype(BF16) for h in heads]
    o = [_dot(s_b[h], v[h]) + _dot(qw[h], state[h].astype(BF16)) for h in heads]
    for h in heads:
        st_ref[h] = state[h] * chunk_decay[h] + _dot_tn(kw[h], v[h])
    for h in heads:
        ms = jnp.mean(o[h] * o[h], axis=-1, keepdims=True)
        y = o[h] * lax.rsqrt(ms + RMS_EPS) * nw_ref[h:h + 1, :]
        o_ref[:, hsl[h]] = (y * _silu(g_ref[:, hsl[h]].astype(F32))).astype(o_ref.dtype)


def _retention(proj, ret_norm_w, b, t):
    c = RET_C
    nt = t // c
    dmat, kws, qws, chunk_decay = _retention_consts(c)
    base = RET_OFF // RET_W

    def col(ci):
        return pl.BlockSpec((c, RET_W), lambda bi, ti: (bi * nt + ti, base + ci))

    def const3(shape):
        return pl.BlockSpec(shape, lambda bi, ti: (0, 0, 0))

    return pl.pallas_call(
        functools.partial(_retention_kernel, chunk_decay=chunk_decay),
        grid=(b, nt),
        in_specs=[col(0), col(1), col(2), col(3),
                  const3(dmat.shape), const3(kws.shape), const3(qws.shape),
                  pl.BlockSpec((RET_HEADS, HEAD_DIM), lambda bi, ti: (0, 0))],
        out_specs=pl.BlockSpec((c, RET_W), lambda bi, ti: (bi * nt + ti, 0)),
        out_shape=jax.ShapeDtypeStruct((b * t, RET_W), BF16),
        scratch_shapes=[pltpu.VMEM((RET_HEADS, HEAD_DIM, HEAD_DIM), F32)],
        compiler_params=pltpu.CompilerParams(
            dimension_semantics=("parallel", "arbitrary"), vmem_limit_bytes=VMEM_LIMIT),
        name="retention",
    )(proj, proj, proj, proj, dmat, kws, qws, ret_norm_w)


def _gdn_kernel(qkv_ref, z_ref, sm_ref, cw_ref, al_ref, dtb_ref, nw_ref, tril_ref, o_ref,
                buf_ref, st_ref, *, nsub):
    c = GDN_C
    d = HEAD_DIM
    tt = nsub * c
    heads = range(GDN_HEADS)

    @pl.when(pl.program_id(1) == 0)
    def _():
        buf_ref[0:HALO, :] = jnp.zeros((HALO, buf_ref.shape[1]), F32)
        st_ref[...] = jnp.zeros_like(st_ref)

    buf_ref[HALO:HALO + tt, :] = qkv_ref[...].astype(F32)

    def conv_act(s, col):
        sl = slice(col * d, (col + 1) * d)
        r0 = HALO + s * c
        acc = cw_ref[GDN_CONV - 1:GDN_CONV, sl] * buf_ref[r0:r0 + c, sl]
        for j in range(GDN_CONV - 1):
            sh = GDN_CONV - 1 - j
            acc = acc + cw_ref[j:j + 1, sl] * buf_ref[r0 - sh:r0 - sh + c, sl]
        return _silu(acc)

    def l2n(t):
        return t * lax.rsqrt(jnp.sum(t * t, axis=-1, keepdims=True) + RMS_EPS)

    sm = sm_ref[...]
    x = sm + dtb_ref[...]
    softplus = jnp.maximum(x, 0.0) + jnp.log1p(jnp.exp(-jnp.abs(x)))
    g = -jnp.exp(al_ref[...]) * softplus
    beta = _sigmoid(sm)
    tril = tril_ref[...]
    gc, gct = [], []
    for s in range(nsub):
        g1, g2, g3 = _split3(g[s * c:(s + 1) * c])
        gcs = _dot(tril, g1) + _dot(tril, g2) + _dot(tril, g3)
        gc.append(gcs)
        gct.append(gcs.T)

    row = lax.broadcasted_iota(jnp.int32, (c, c), 0)
    coli = lax.broadcasted_iota(jnp.int32, (c, c), 1)
    causal = row >= coli
    strict = row > coli
    eye = jnp.where(row == coli, 1.0, 0.0).astype(F32)
    diag8 = strict & ((row >> 3) == (coli >> 3))
    merge_masks = [((row >> (lv + 1)) == (coli >> (lv + 1))) & ((row >> lv) != (coli >> lv)) & strict
                   for lv in range(3, 7)]

    units = [(s, h) for s in range(nsub) for h in heads]
    un = range(len(units))
    q = [l2n(conv_act(s, h)) * SCALE for s, h in units]
    k = [l2n(conv_act(s, GDN_HEADS + h)) for s, h in units]
    v = [conv_act(s, 2 * GDN_HEADS + h) for s, h in units]
    gcol = [jnp.broadcast_to(gc[s][:, h:h + 1], (c, c)) for s, h in units]
    dm = [jnp.exp(jnp.where(causal, gcol[u] - gct[s][h:h + 1, :], NEG_BIG))
          for u, (s, h) in enumerate(units)]
    bb = [jnp.broadcast_to(beta[s * c:(s + 1) * c, GDN_HEADS + h:GDN_HEADS + h + 1], (c, d))
          for s, h in units]
    eg = [jnp.exp(gcol[u]) for u in un]
    glast = [gcol[u][c - 1:c, :] for u in un]
    ek = [jnp.exp(glast[u] - gcol[u]) for u in un]
    kb = [k[u] * bb[u] for u in un]
    k_b = [k[u].astype(BF16) for u in un]

    sc = [_dot_nt(jnp.concatenate([kb[u], q[u]], axis=0).astype(BF16), k_b[u]) for u in un]
    mm = [jnp.where(strict, sc[u][:c] * dm[u], 0.0) for u in un]
    qk_b = [(sc[u][c:] * dm[u]).astype(BF16) for u in un]

    md = [jnp.where(diag8, mm[u], 0.0) for u in un]
    md_b = [md[u].astype(BF16) for u in un]
    x0 = [eye - md[u] for u in un]
    md2_b = [_dot(md_b[u], md_b[u]).astype(BF16) for u in un]
    t1 = [_dot(md2_b[u], jnp.concatenate([x0[u].astype(BF16), md2_b[u]], axis=-1)) for u in un]
    x1 = [x0[u] + t1[u][:, :c] for u in un]
    md4_b = [t1[u][:, c:].astype(BF16) for u in un]
    xi = [x1[u] + _dot(md4_b[u], x1[u].astype(BF16)) for u in un]
    for mask in merge_masks:
        mo_b = [jnp.where(mask, mm[u], 0.0).astype(BF16) for u in un]
        xi_b = [xi[u].astype(BF16) for u in un]
        y_b = [_dot(xi_b[u], mo_b[u]).astype(BF16) for u in un]
        xi = [xi[u] - _dot(y_b[u], xi_b[u]) for u in un]
    xc_b = [(xi[u] - eye).astype(BF16) for u in un]

    rhs = [jnp.concatenate([kb[u] * eg[u], v[u] * bb[u]], axis=-1) for u in un]
    wu = [rhs[u] + _dot(xc_b[u], rhs[u].astype(BF16)) for u in un]
    wq_b = [jnp.concatenate([wu[u][:, :d], q[u] * eg[u]], axis=0).astype(BF16) for u in un]
    kg_b = [(k[u] * ek[u]).astype(BF16) for u in un]
    elast = [jnp.exp(glast[u]) for u in un]

    state = [st_ref[h] for h in heads]
    for s in range(nsub):
        us = [s * GDN_HEADS + h for h in heads]
        state_b = [state[h].astype(BF16) for h in heads]
        ws = [_dot(wq_b[us[h]], state_b[h]) for h in heads]
        v_new_b = [(wu[us[h]][:, d:] - ws[h][:c]).astype(BF16) for h in heads]
        o = [ws[h][c:] + _dot(qk_b[us[h]], v_new_b[h]) for h in heads]
        state = [state[h] * elast[us[h]] + _dot_tn(kg_b[us[h]], v_new_b[h]) for h in heads]
        for h in heads:
            sl = slice(h * d, (h + 1) * d)
            rows = slice(s * c, (s + 1) * c)
            ms = jnp.mean(o[h] * o[h], axis=-1, keepdims=True)
            y = o[h] * lax.rsqrt(ms + RMS_EPS) * nw_ref[...]
            o_ref[rows, sl] = (y * _silu(z_ref[rows, sl].astype(F32))).astype(o_ref.dtype)
    for h in heads:
        st_ref[h] = state[h]

    buf_ref[0:HALO, :] = buf_ref[tt:tt + HALO, :]


def _gdn(proj, small, conv_w, a_log, dt_bias, gdn_norm_w, b, t, nsub=2):
    c = GDN_C
    tt = nsub * c
    nt = t // tt
    qkv_w = 3 * GDN_W
    pad = SMALL_W - GDN_HEADS
    al_row = jnp.pad(a_log.astype(F32), (0, pad)).reshape(1, SMALL_W)
    dtb_row = jnp.pad(dt_bias.astype(F32), (0, pad)).reshape(1, SMALL_W)
    tril = jnp.asarray(np.tril(np.ones((c, c), np.float32)), BF16)
    return pl.pallas_call(
        functools.partial(_gdn_kernel, nsub=nsub),
        grid=(b, nt),
        in_specs=[
            pl.BlockSpec((tt, qkv_w), lambda bi, ti: (bi * nt + ti, GDN_OFF // qkv_w)),
            pl.BlockSpec((tt, GDN_W), lambda bi, ti: (bi * nt + ti, (GDN_OFF + qkv_w) // GDN_W)),
            pl.BlockSpec((tt, SMALL_W), lambda bi, ti: (bi * nt + ti, 0)),
            pl.BlockSpec((GDN_CONV, qkv_w), lambda bi, ti: (0, 0)),
            pl.BlockSpec((1, SMALL_W), lambda bi, ti: (0, 0)),
            pl.BlockSpec((1, SMALL_W), lambda bi, ti: (0, 0)),
            pl.BlockSpec((1, HEAD_DIM), lambda bi, ti: (0, 0)),
            pl.BlockSpec((c, c), lambda bi, ti: (0, 0)),
        ],
        out_specs=pl.BlockSpec((tt, GDN_W), lambda bi, ti: (bi * nt + ti, 0)),
        out_shape=jax.ShapeDtypeStruct((b * t, GDN_W), BF16),
        scratch_shapes=[pltpu.VMEM((HALO + tt, qkv_w), F32),
                        pltpu.VMEM((GDN_HEADS, HEAD_DIM, HEAD_DIM), F32)],
        compiler_params=pltpu.CompilerParams(
            dimension_semantics=("parallel", "arbitrary"), vmem_limit_bytes=VMEM_LIMIT),
        name="gdn",
    )(proj, proj, small, conv_w, al_row, dtb_row, gdn_norm_w.reshape(1, HEAD_DIM), tril)


def _moba_key_features(t):
    pos = np.arange(t)
    blk = pos // MOBA_BLOCK
    off = pos % MOBA_BLOCK
    f = np.zeros((t, HEAD_DIM), np.float32)
    f[pos, blk] = 1.0
    f[:, SEL_LANES] = 1.0
    for p in range(3):
        f[:, SEL_LANES + 1 + p] = blk
        f[:, SEL_LANES + 4 + p] = off // 16
        f[:, SEL_LANES + 7 + p] = off % 16
    return jnp.asarray(f, BF16)


def _moba_query_features():
    hh = MOBA_HEADS
    slopes = np.exp2(-8.0 * np.arange(1, hh + 1, dtype=np.float64) / hh).astype(np.float32)
    s = jnp.asarray(slopes, F32)
    s1 = s.astype(BF16).astype(F32)
    s2 = (s - s1).astype(BF16).astype(F32)
    s3 = (s - s1 - s2).astype(BF16).astype(F32)
    f = jnp.zeros((hh, HEAD_DIM), F32)
    f = f.at[:, SEL_LANES].set(-s * MOBA_BLOCK)
    for p, sp in enumerate((s1, s2, s3)):
        f = f.at[:, SEL_LANES + 1 + p].set(sp * MOBA_BLOCK)
        f = f.at[:, SEL_LANES + 4 + p].set(sp * 16.0)
        f = f.at[:, SEL_LANES + 7 + p].set(sp)
    return f


def _moba_prep_kernel(k_ref, v_ref, nw_ref, pf_ref, ka_ref, va_ref, km_ref):
    ones = jnp.ones((k_ref.shape[0], HEAD_DIM), BF16)
    for h in range(MOBA_HEADS):
        sl = slice(h * HEAD_DIM, (h + 1) * HEAD_DIM)
        k = k_ref[:, sl].astype(F32)
        ms = jnp.mean(k * k, axis=-1, keepdims=True)
        kn = k * lax.rsqrt(ms + RMS_EPS) * nw_ref[...]
        ka_ref[0, h, :, 0:HEAD_DIM] = kn.astype(BF16)
        ka_ref[0, h, :, HEAD_DIM:2 * HEAD_DIM] = pf_ref[...]
        km_ref[0, 0, :, sl] = jnp.mean(kn, axis=0, keepdims=True)
        va_ref[0, h, :, 0:HEAD_DIM] = v_ref[:, sl]
        va_ref[0, h, :, HEAD_DIM:2 * HEAD_DIM] = ones


def _moba_prep(proj, k_norm_w, key_feat, b, t):
    blk = MOBA_BLOCK
    nb = t // blk
    kc = (MOBA_OFF + MOBA_W) // MOBA_W
    aug = pl.BlockSpec((1, MOBA_HEADS, blk, 2 * HEAD_DIM), lambda bi, ti: (bi, 0, ti, 0))
    aug_shape = jax.ShapeDtypeStruct((b, MOBA_HEADS, t, 2 * HEAD_DIM), BF16)
    return pl.pallas_call(
        _moba_prep_kernel,
        grid=(b, nb),
        in_specs=[
            pl.BlockSpec((blk, MOBA_W), lambda bi, ti: (bi * nb + ti, kc)),
            pl.BlockSpec((blk, MOBA_W), lambda bi, ti: (bi * nb + ti, kc + 1)),
            pl.BlockSpec((1, HEAD_DIM), lambda bi, ti: (0, 0)),
            pl.BlockSpec((blk, HEAD_DIM), lambda bi, ti: (ti, 0)),
        ],
        out_specs=[aug, aug, pl.BlockSpec((1, 1, 1, MOBA_W), lambda bi, ti: (bi, ti, 0, 0))],
        out_shape=[aug_shape, aug_shape, jax.ShapeDtypeStruct((b, nb, 1, MOBA_W), F32)],
        compiler_params=pltpu.CompilerParams(
            dimension_semantics=("parallel", "parallel"), vmem_limit_bytes=VMEM_LIMIT),
        name="moba_prep",
    )(proj, proj, k_norm_w.reshape(1, HEAD_DIM), key_feat)


def _moba_qprep_kernel(q_ref, km_ref, qnw_ref, qf_ref, qa_ref):
    tq = q_ref.shape[0]
    lane = lax.broadcasted_iota(jnp.int32, (tq, HEAD_DIM), 1)
    lane_f = lane.astype(F32)
    rowi = lax.broadcasted_iota(jnp.int32, (tq, HEAD_DIM), 0)
    own = (pl.program_id(1) * tq + rowi) // MOBA_BLOCK
    past = lane < own
    own_coef = jnp.where(lane == SEL_LANES, own.astype(F32), 1.0)
    heads = range(MOBA_HEADS)
    hsl = [slice(h * HEAD_DIM, (h + 1) * HEAD_DIM) for h in heads]

    def qnorm(h):
        q = q_ref[:, hsl[h]].astype(F32)
        ms = jnp.mean(q * q, axis=-1, keepdims=True)
        return q * lax.rsqrt(ms + RMS_EPS) * qnw_ref[...]

    qn = [qnorm(h) for h in heads]
    for h in heads:
        qa_ref[0, h, :, 0:HEAD_DIM] = (qn[h] * SCALE).astype(BF16)

    def gate_scores(h):
        km = km_ref[0, :, hsl[h]]
        q1 = qn[h].astype(BF16)
        q2 = (qn[h] - q1.astype(F32)).astype(BF16)
        k1 = km.astype(BF16)
        k2 = (km - k1.astype(F32)).astype(BF16)
        return jnp.where(past, _dot_nt(q1, k1) + _dot_nt(q1, k2) + _dot_nt(q2, k1), -jnp.inf)

    gate = [gate_scores(h) for h in heads]
    taken = [jnp.zeros((tq, HEAD_DIM), F32) for _ in heads]
    for _ in range(MOBA_TOP_K):
        gm = [jnp.where(taken[h] > 0.0, -jnp.inf, gate[h]) for h in heads]
        mx = [jnp.max(gm[h], axis=-1, keepdims=True) for h in heads]
        cand = [jnp.where(taken[h] > 0.0, float(HEAD_DIM),
                          jnp.where(gm[h] == mx[h], lane_f, float(HEAD_DIM))) for h in heads]
        idx = [jnp.min(cand[h], axis=-1, keepdims=True) for h in heads]
        taken = [jnp.where(lane_f == idx[h], 1.0, taken[h]) for h in heads]
    for h in heads:
        sel_bias = jnp.where(lane == own, 0.0,
                             jnp.where(past, jnp.where(taken[h] > 0.0, 0.0, NEG_BIG), NEG_BIG))
        extra = jnp.where(lane < SEL_LANES, sel_bias, qf_ref[h:h + 1, :] * own_coef)
        qa_ref[0, h, :, HEAD_DIM:2 * HEAD_DIM] = extra.astype(BF16)


def _moba_qprep(proj, k_mean, q_norm_w, q_feat, b, t, tq=256):
    nq = t // tq
    return pl.pallas_call(
        _moba_qprep_kernel,
        grid=(b, nq),
        in_specs=[
            pl.BlockSpec((tq, MOBA_W), lambda bi, ti: (bi * nq + ti, MOBA_OFF // MOBA_W)),
            pl.BlockSpec((1, HEAD_DIM, MOBA_W), lambda bi, ti: (bi, 0, 0)),
            pl.BlockSpec((1, HEAD_DIM), lambda bi, ti: (0, 0)),
            pl.BlockSpec((MOBA_HEADS, HEAD_DIM), lambda bi, ti: (0, 0)),
        ],
        out_specs=pl.BlockSpec((1, MOBA_HEADS, tq, 2 * HEAD_DIM), lambda bi, ti: (bi, 0, ti, 0)),
        out_shape=jax.ShapeDtypeStruct((b, MOBA_HEADS, t, 2 * HEAD_DIM), BF16),
        compiler_params=pltpu.CompilerParams(
            dimension_semantics=("parallel", "parallel"), vmem_limit_bytes=VMEM_LIMIT),
        name="moba_qprep",
    )(proj, k_mean, q_norm_w.reshape(1, HEAD_DIM), q_feat)


def _moba_attn_kernel(qa_ref, ka_ref, va_ref, z_ref, o_ref, s_ref, m_ref, acc_ref, *, group, hp, tq):
    blk = MOBA_BLOCK
    d = HEAD_DIM
    gw = group * blk
    qi = pl.program_id(2)
    n_full = (qi * tq) // gw
    heads = range(hp)

    m_ref[...] = jnp.full(m_ref.shape, NEG_BIG, F32)
    acc_ref[...] = jnp.zeros(acc_ref.shape, F32)

    def store_scores(slot, g):
        off = pl.multiple_of(g * gw, gw)
        for h in heads:
            s_ref[slot, h] = _dot_nt(qa_ref[0, h], ka_ref[0, h, pl.ds(off, gw), :])

    def update(s, g):
        off = pl.multiple_of(g * gw, gw)
        m_prev = [m_ref[h] for h in heads]
        m_new = [jnp.maximum(m_prev[h], jnp.max(s[h], axis=-1, keepdims=True)) for h in heads]
        alpha = [jnp.exp(m_prev[h] - m_new[h]) for h in heads]
        p = [jnp.exp(s[h] - jnp.concatenate([m_new[h]] * (gw // d), axis=-1)).astype(BF16)
             for h in heads]
        for h in heads:
            m_ref[h] = m_new[h]
        pv = [_dot(p[h], va_ref[0, h, pl.ds(off, gw), :]) for h in heads]
        for h in heads:
            acc_ref[h] = jnp.concatenate([alpha[h], alpha[h]], axis=-1) * acc_ref[h] + pv[h]

    store_scores(0, 0)

    def body(g, carry):
        slot = g % 2
        s = [s_ref[slot, h] for h in heads]
        update(s, g)
        store_scores(1 - slot, g + 1)
        return carry

    lax.fori_loop(0, n_full, body, 0)

    qpos = qi * tq + lax.broadcasted_iota(jnp.int32, (tq, gw), 0)
    kpos = n_full * gw + lax.broadcasted_iota(jnp.int32, (tq, gw), 1)
    visible = kpos <= qpos
    slot = n_full % 2
    update([jnp.where(visible, s_ref[slot, h], NEG_BIG) for h in heads], n_full)

    for h in heads:
        sl = slice(h * d, (h + 1) * d)
        o = acc_ref[h, :, 0:d] / acc_ref[h, :, d:2 * d]
        o_ref[:, sl] = (o * _silu(z_ref[:, sl].astype(F32))).astype(o_ref.dtype)


def _moba_attn(proj, q_aug, k_aug, v_aug, b, t, group=4, hp=2, tq=512):
    blk = MOBA_BLOCK
    nb = t // blk
    nq = t // tq
    w = hp * HEAD_DIM
    zc = (MOBA_OFF + 3 * MOBA_W) // w
    assert nb % group == 0 and MOBA_HEADS % hp == 0 and MOBA_W % w == 0 and MOBA_OFF % w == 0
    assert tq % blk == 0 and (group * blk) % tq == 0
    resident = pl.BlockSpec((1, hp, t, 2 * HEAD_DIM), lambda bi, hi, qi: (bi, hi, 0, 0))
    return pl.pallas_call(
        functools.partial(_moba_attn_kernel, group=group, hp=hp, tq=tq),
        grid=(b, MOBA_HEADS // hp, nq),
        in_specs=[
            pl.BlockSpec((1, hp, tq, 2 * HEAD_DIM), lambda bi, hi, qi: (bi, hi, qi, 0)),
            resident,
            resident,
            pl.BlockSpec((tq, w), lambda bi, hi, qi: (bi * nq + qi, zc + hi)),
        ],
        out_specs=pl.BlockSpec((tq, w), lambda bi, hi, qi: (bi * nq + qi, hi)),
        out_shape=jax.ShapeDtypeStruct((b * t, MOBA_W), BF16),
        scratch_shapes=[pltpu.VMEM((2, hp, tq, group * blk), F32),
                        pltpu.VMEM((hp, tq, HEAD_DIM), F32),
                        pltpu.VMEM((hp, tq, 2 * HEAD_DIM), F32)],
        compiler_params=pltpu.CompilerParams(
            dimension_semantics=("parallel", "parallel", "arbitrary"), vmem_limit_bytes=VMEM_LIMIT),
        name="moba_attn",
    )(q_aug, k_aug, v_aug, proj)


def _layer(x2d, h2d, b, t, layer, w_a, w_b, w_small, conv_w, a_log, dt_bias, ret_norm_w, gdn_norm_w,
           q_norm_w, k_norm_w, w_out, next_norm_w, key_feat, q_feat):
    nb = t // MOBA_BLOCK
    proj, small = _in_proj(h2d, w_a, w_b, w_small, layer)
    y_ret = _retention(proj, ret_norm_w, b, t)
    y_gdn = _gdn(proj, small, conv_w, a_log, dt_bias, gdn_norm_w, b, t)
    k_aug, v_aug, k_mean = _moba_prep(proj, k_norm_w, key_feat, b, t)
    k_mean = jnp.pad(k_mean.reshape(b, nb, MOBA_W), ((0, 0), (0, HEAD_DIM - nb), (0, 0)))
    q_aug = _moba_qprep(proj, k_mean, q_norm_w, q_feat, b, t)
    y_moba = _moba_attn(proj, q_aug, k_aug, v_aug, b, t)
    return _out_proj(x2d, y_ret, y_gdn, y_moba, w_out, layer, next_norm_w)


def kernel(x, norm_w, w_in, conv_w, a_log, dt_bias, ret_norm_w, gdn_norm_w, q_norm_w, k_norm_w, w_out):
    b, t, d = x.shape
    depth = w_in.shape[0]
    assert t % MOBA_BLOCK == 0 and t // MOBA_BLOCK <= SEL_LANES
    small_lo = 4 * RET_W + 4 * GDN_W
    moba_lo = small_lo + 2 * GDN_HEADS
    w_b = w_in[:, :, moba_lo:]
    w_small = jnp.pad(w_in[:, :, small_lo:moba_lo], ((0, 0), (0, 0), (0, SMALL_W - 2 * GDN_HEADS)))
    w_out_b = w_out.astype(BF16)
    key_feat = _moba_key_features(t)
    q_feat = _moba_query_features()
    x2d = x.reshape(b * t, d)
    h2d = _norm(x2d, norm_w[0])
    for layer in range(depth):
        next_norm_w = norm_w[layer + 1] if layer + 1 < depth else None
        x2d, h2d = _layer(x2d, h2d, b, t, layer, w_in, w_b, w_small, conv_w[layer], a_log[layer],
                          dt_bias[layer], ret_norm_w[layer], gdn_norm_w[layer], q_norm_w[layer],
                          k_norm_w[layer], w_out_b, next_norm_w, key_feat, q_feat)
    return x2d.reshape(b, t, d)
```

```python
import functools

import numpy as np
import jax
import jax.numpy as jnp
from jax import lax
from jax.experimental import pallas as pl
from jax.experimental.pallas import tpu as pltpu

F32 = jnp.float32
BF16 = jnp.bfloat16

HEAD_DIM = 128
RET_HEADS = 4
GDN_HEADS = 6
MOBA_HEADS = 6
RET_W = RET_HEADS * HEAD_DIM
GDN_W = GDN_HEADS * HEAD_DIM
MOBA_W = MOBA_HEADS * HEAD_DIM
GDN_CONV = 4
MOBA_BLOCK = 256
MOBA_TOP_K = 3
RMS_EPS = 1e-6
SCALE = HEAD_DIM ** -0.5

GDN_OFF = 0
MOBA_OFF = 4 * GDN_W
RET_OFF = 4 * GDN_W + 4 * MOBA_W
MAIN_W = RET_OFF + 4 * RET_W
SMALL_W = 128

RET_C = 256
GDN_C = 128
HALO = 8

NEG_BIG = -1e30
SEL_LANES = 32
VMEM_LIMIT = 56 * 1024 * 1024


def _dot(a, b):
    return jnp.dot(a, b, preferred_element_type=F32)


def _dot_nt(a, b):
    return lax.dot_general(a, b, (((1,), (1,)), ((), ())), preferred_element_type=F32)


def _dot_tn(a, b):
    return lax.dot_general(a, b, (((0,), (0,)), ((), ())), preferred_element_type=F32)


def _sigmoid(x):
    return 1.0 / (1.0 + jnp.exp(-x))


def _silu(x):
    return x * _sigmoid(x)


def _split3(x):
    x1 = x.astype(BF16)
    r = x - x1.astype(F32)
    x2 = r.astype(BF16)
    x3 = (r - x2.astype(F32)).astype(BF16)
    return x1, x2, x3


def _rmsnorm_rows(x, nw):
    ms = jnp.mean(x * x, axis=-1, keepdims=True)
    return (x * lax.rsqrt(ms + RMS_EPS) * nw).astype(BF16)


def _norm_kernel(x_ref, nw_ref, h_ref):
    h_ref[...] = _rmsnorm_rows(x_ref[...], nw_ref[...])


def _norm(x2d, norm_w, tm=512):
    m, d = x2d.shape
    return pl.pallas_call(
        _norm_kernel,
        grid=(m // tm,),
        in_specs=[pl.BlockSpec((tm, d), lambda i: (i, 0)), pl.BlockSpec((1, d), lambda i: (0, 0))],
        out_specs=pl.BlockSpec((tm, d), lambda i: (i, 0)),
        out_shape=jax.ShapeDtypeStruct((m, d), BF16),
        compiler_params=pltpu.CompilerParams(
            dimension_semantics=("parallel",), vmem_limit_bytes=VMEM_LIMIT),
        name="rmsnorm",
    )(x2d, norm_w.reshape(1, d))


def _in_proj_kernel(h_ref, wa_ref, wb_ref, ws_ref, o_ref, os_ref, *, na):
    j = pl.program_id(1)

    @pl.when(j == 0)
    def _():
        os_ref[...] = _dot_nt(h_ref[...], ws_ref[0])

    @pl.when(j < na)
    def _():
        o_ref[...] = _dot_nt(h_ref[...], wa_ref[0]).astype(o_ref.dtype)

    @pl.when(j >= na)
    def _():
        o_ref[...] = _dot_nt(h_ref[...], wb_ref[0]).astype(o_ref.dtype)


def _in_proj(h2d, w_a, w_b, w_small, layer, tm=2048, tn=512):
    m, d = h2d.shape
    nr = 4 * RET_W // tn
    na = (4 * RET_W + 4 * GDN_W) // tn
    nb = 4 * MOBA_W // tn
    assert 4 * RET_W % tn == 0 and 4 * GDN_W % tn == 0 and 4 * MOBA_W % tn == 0 and m % tm == 0
    return pl.pallas_call(
        functools.partial(_in_proj_kernel, na=na),
        grid=(m // tm, na + nb),
        in_specs=[
            pl.BlockSpec((tm, d), lambda i, j: (i, 0)),
            pl.BlockSpec((1, tn, d), lambda i, j: (layer, jnp.minimum(j, na - 1), 0)),
            pl.BlockSpec((1, tn, d), lambda i, j: (layer, jnp.maximum(j - na, 0), 0)),
            pl.BlockSpec((1, SMALL_W, d), lambda i, j: (layer, 0, 0)),
        ],
        out_specs=[
            pl.BlockSpec((tm, tn), lambda i, j: (i, jnp.where(j < nr, j + (na - nr) + nb, j - nr))),
            pl.BlockSpec((tm, SMALL_W), lambda i, j: (i, 0)),
        ],
        out_shape=[
            jax.ShapeDtypeStruct((m, MAIN_W), BF16),
            jax.ShapeDtypeStruct((m, SMALL_W), F32),
        ],
        compiler_params=pltpu.CompilerParams(
            dimension_semantics=("parallel", "arbitrary"), vmem_limit_bytes=VMEM_LIMIT),
        name="in_proj",
    )(h2d, w_a, w_b, w_small)


def _out_proj_kernel(x_ref, yr_ref, yg_ref, ym_ref, w_ref, *rest, tn, with_norm):
    if with_norm:
        nw_ref, o_ref, h_ref = rest
    else:
        (o_ref,) = rest
    n = o_ref.shape[1]
    yr = yr_ref[...]
    yg = yg_ref[...]
    ym = ym_ref[...]
    for c in range(0, n, tn):
        acc = x_ref[:, c:c + tn]
        acc = acc + _dot(yr, w_ref[0, 0:RET_W, c:c + tn])
        acc = acc + _dot(yg, w_ref[0, RET_W:RET_W + GDN_W, c:c + tn])
        acc = acc + _dot(ym, w_ref[0, RET_W + GDN_W:RET_W + GDN_W + MOBA_W, c:c + tn])
        o_ref[:, c:c + tn] = acc
    if with_norm:
        h_ref[...] = _rmsnorm_rows(o_ref[...], nw_ref[...])


def _out_proj(x2d, y_ret, y_gdn, y_moba, w_out, layer, next_norm_w=None, tm=512, tn=512):
    m, d = x2d.shape
    with_norm = next_norm_w is not None
    in_specs = [
        pl.BlockSpec((tm, d), lambda i: (i, 0)),
        pl.BlockSpec((tm, RET_W), lambda i: (i, 0)),
        pl.BlockSpec((tm, GDN_W), lambda i: (i, 0)),
        pl.BlockSpec((tm, MOBA_W), lambda i: (i, 0)),
        pl.BlockSpec((1,) + w_out.shape[1:], lambda i: (layer, 0, 0)),
    ]
    args = [x2d, y_ret, y_gdn, y_moba, w_out]
    out_specs = [pl.BlockSpec((tm, d), lambda i: (i, 0))]
    out_shape = [jax.ShapeDtypeStruct((m, d), F32)]
    if with_norm:
        in_specs.append(pl.BlockSpec((1, d), lambda i: (0, 0)))
        args.append(next_norm_w.reshape(1, d))
        out_specs.append(pl.BlockSpec((tm, d), lambda i: (i, 0)))
        out_shape.append(jax.ShapeDtypeStruct((m, d), BF16))
    res = pl.pallas_call(
        functools.partial(_out_proj_kernel, tn=tn, with_norm=with_norm),
        grid=(m // tm,),
        in_specs=in_specs,
        out_specs=out_specs,
        out_shape=out_shape,
        compiler_params=pltpu.CompilerParams(
            dimension_semantics=("parallel",), vmem_limit_bytes=VMEM_LIMIT),
        name="out_proj",
    )(*args)
    return (res[0], res[1]) if with_norm else (res[0], None)


def _retention_consts(c):
    h = np.arange(RET_HEADS, dtype=np.float64)
    log_gamma = np.log1p(-np.exp2(-5.0 - h))
    pos = np.arange(c, dtype=np.float64)
    diff = pos[:, None] - pos[None, :]
    dmat = np.where(diff >= 0, np.exp(np.maximum(diff, 0.0) * log_gamma[:, None, None]), 0.0) * SCALE
    kws = np.exp((c - 1 - pos)[None, :] * log_gamma[:, None])
    qws = np.exp((pos + 1.0)[None, :] * log_gamma[:, None]) * SCALE
    kws = np.broadcast_to(kws[:, :, None], (RET_HEADS, c, HEAD_DIM))
    qws = np.broadcast_to(qws[:, :, None], (RET_HEADS, c, HEAD_DIM))
    chunk_decay = [float(v) for v in np.exp(c * log_gamma)]
    return (jnp.asarray(dmat, F32), jnp.asarray(kws, F32), jnp.asarray(qws, F32), chunk_decay)


def _retention_kernel(q_ref, k_ref, v_ref, g_ref, dm_ref, kws_ref, qws_ref, nw_ref, o_ref, st_ref,
                      *, chunk_decay):
    @pl.when(pl.program_id(1) == 0)
    def _():
        st_ref[...] = jnp.zeros_like(st_ref)

    heads = range(RET_HEADS)
    hsl = [slice(h * HEAD_DIM, (h + 1) * HEAD_DIM) for h in heads]
    q = [q_ref[:, hsl[h]] for h in heads]
    k = [k_ref[:, hsl[h]] for h in heads]
    v = [v_ref[:, hsl[h]] for h in heads]
    s_b = [(_dot_nt(q[h], k[h]) * dm_ref[h]).astype(BF16) for h in heads]
    state = [st_ref[h] for h in heads]
    qw = [(q[h].astype(F32) * qws_ref[h]).astype(BF16) for h in heads]
    kw = [(k[h].astype(F32) * kws_ref[h]).astype(BF16) for h in heads]
    o = [_dot(s_b[h], v[h]) + _dot(qw[h], state[h].astype(BF16)) for h in heads]
    for h in heads:
        st_ref[h] = state[h] * chunk_decay[h] + _dot_tn(kw[h], v[h])
    for h in heads:
        ms = jnp.mean(o[h] * o[h], axis=-1, keepdims=True)
        y = o[h] * lax.rsqrt(ms + RMS_EPS) * nw_ref[h:h + 1, :]
        o_ref[:, hsl[h]] = (y * _silu(g_ref[:, hsl[h]].astype(F32))).astype(o_ref.dtype)


def _retention(proj, ret_norm_w, b, t):
    c = RET_C
    nt = t // c
    dmat, kws, qws, chunk_decay = _retention_consts(c)
    base = RET_OFF // RET_W

    def col(ci):
        return pl.BlockSpec((c, RET_W), lambda bi, ti: (bi * nt + ti, base + ci))

    def const3(shape):
        return pl.BlockSpec(shape, lambda bi, ti: (0, 0, 0))

    return pl.pallas_call(
        functools.partial(_retention_kernel, chunk_decay=chunk_decay),
        grid=(b, nt),
        in_specs=[col(0), col(1), col(2), col(3),
                  const3(dmat.shape), const3(kws.shape), const3(qws.shape),
                  pl.BlockSpec((RET_HEADS, HEAD_DIM), lambda bi, ti: (0, 0))],
        out_specs=pl.BlockSpec((c, RET_W), lambda bi, ti: (bi * nt + ti, 0)),
        out_shape=jax.ShapeDtypeStruct((b * t, RET_W), BF16),
        scratch_shapes=[pltpu.VMEM((RET_HEADS, HEAD_DIM, HEAD_DIM), F32)],
        compiler_params=pltpu.CompilerParams(
            dimension_semantics=("parallel", "arbitrary"), vmem_limit_bytes=VMEM_LIMIT),
        name="retention",
    )(proj, proj, proj, proj, dmat, kws, qws, ret_norm_w)


def _gdn_kernel(qkv_ref, z_ref, sm_ref, cw_ref, al_ref, dtb_ref, nw_ref, tril_ref, o_ref,
                buf_ref, st_ref, *, nsub):
    c = GDN_C
    d = HEAD_DIM
    tt = nsub * c
    heads = range(GDN_HEADS)

    @pl.when(pl.program_id(1) == 0)
    def _():
        buf_ref[0:HALO, :] = jnp.zeros((HALO, buf_ref.shape[1]), F32)
        st_ref[...] = jnp.zeros_like(st_ref)

    buf_ref[HALO:HALO + tt, :] = qkv_ref[...].astype(F32)

    def conv_act(s, col):
        sl = slice(col * d, (col + 1) * d)
        r0 = HALO + s * c
        acc = cw_ref[GDN_CONV - 1:GDN_CONV, sl] * buf_ref[r0:r0 + c, sl]
        for j in range(GDN_CONV - 1):
            sh = GDN_CONV - 1 - j
            acc = acc + cw_ref[j:j + 1, sl] * buf_ref[r0 - sh:r0 - sh + c, sl]
        return _silu(acc)

    def l2n(t):
        return t * lax.rsqrt(jnp.sum(t * t, axis=-1, keepdims=True) + RMS_EPS)

    sm = sm_ref[...]
    x = sm + dtb_ref[...]
    softplus = jnp.maximum(x, 0.0) + jnp.log1p(jnp.exp(-jnp.abs(x)))
    g = -jnp.exp(al_ref[...]) * softplus
    beta = _sigmoid(sm)
    tril = tril_ref[...]
    gc, gct = [], []
    for s in range(nsub):
        g1, g2, g3 = _split3(g[s * c:(s + 1) * c])
        gcs = _dot(tril, g1) + _dot(tril, g2) + _dot(tril, g3)
        gc.append(gcs)
        gct.append(gcs.T)

    row = lax.broadcasted_iota(jnp.int32, (c, c), 0)
    coli = lax.broadcasted_iota(jnp.int32, (c, c), 1)
    causal = row >= coli
    strict = row > coli
    eye = jnp.where(row == coli, 1.0, 0.0).astype(F32)
    diag8 = strict & ((row >> 3) == (coli >> 3))
    merge_masks = [((row >> (lv + 1)) == (coli >> (lv + 1))) & ((row >> lv) != (coli >> lv)) & strict
                   for lv in range(3, 7)]

    units = [(s, h) for s in range(nsub) for h in heads]
    un = range(len(units))
    q = [l2n(conv_act(s, h)) * SCALE for s, h in units]
    k = [l2n(conv_act(s, GDN_HEADS + h)) for s, h in units]
    v = [conv_act(s, 2 * GDN_HEADS + h) for s, h in units]
    gcol = [jnp.broadcast_to(gc[s][:, h:h + 1], (c, c)) for s, h in units]
    dm = [jnp.exp(jnp.where(causal, gcol[u] - gct[s][h:h + 1, :], NEG_BIG))
          for u, (s, h) in enumerate(units)]
    bb = [jnp.broadcast_to(beta[s * c:(s + 1) * c, GDN_HEADS + h:GDN_HEADS + h + 1], (c, d))
          for s, h in units]
    eg = [jnp.exp(gcol[u]) for u in un]
    glast = [gcol[u][c - 1:c, :] for u in un]
    ek = [jnp.exp(glast[u] - gcol[u]) for u in un]
    kb = [k[u] * bb[u] for u in un]
    k_b = [k[u].astype(BF16) for u in un]

    sc = [_dot_nt(jnp.concatenate([kb[u], q[u]], axis=0).astype(BF16), k_b[u]) for u in un]
    mm = [jnp.where(strict, sc[u][:c] * dm[u], 0.0) for u in un]
    qk_b = [(sc[u][c:] * dm[u]).astype(BF16) for u in un]

    md = [jnp.where(diag8, mm[u], 0.0) for u in un]
    md_b = [md[u].astype(BF16) for u in un]
    x0 = [eye - md[u] for u in un]
    md2_b = [_dot(md_b[u], md_b[u]).astype(BF16) for u in un]
    t1 = [_dot(md2_b[u], jnp.concatenate([x0[u].astype(BF16), md2_b[u]], axis=-1)) for u in un]
    x1 = [x0[u] + t1[u][:, :c] for u in un]
    md4_b = [t1[u][:, c:].astype(BF16) for u in un]
    xi = [x1[u] + _dot(md4_b[u], x1[u].astype(BF16)) for u in un]
    for mask in merge_masks:
        mo_b = [jnp.where(mask, mm[u], 0.0).astype(BF16) for u in un]
        xi_b = [xi[u].astype(BF16) for u in un]
        y_b = [_dot(xi_b[u], mo_b[u]).astype(BF16) for u in un]
        xi = [xi[u] - _dot(y_b[u], xi_b[u]) for u in un]
    xc_b = [(xi[u] - eye).astype(BF16) for u in un]

    rhs = [jnp.concatenate([kb[u] * eg[u], v[u] * bb[u]], axis=-1) for u in un]
    wu = [rhs[u] + _dot(xc_b[u], rhs[u].astype(BF16)) for u in un]
    wq_b = [jnp.concatenate([wu[u][:, :d], q[u] * eg[u]], axis=0).astype(BF16) for u in un]
    kg_b = [(k[u] * ek[u]).astype(BF16) for u in un]
    elast = [jnp.exp(glast[u]) for u in un]

    state = [st_ref[h] for h in heads]
    for s in range(nsub):
        us = [s * GDN_HEADS + h for h in heads]
        state_b = [state[h].astype(BF16) for h in heads]
        ws = [_dot(wq_b[us[h]], state_b[h]) for h in heads]
        v_new_b = [(wu[us[h]][:, d:] - ws[h][:c]).astype(BF16) for h in heads]
        o = [ws[h][c:] + _dot(qk_b[us[h]], v_new_b[h]) for h in heads]
        state = [state[h] * elast[us[h]] + _dot_tn(kg_b[us[h]], v_new_b[h]) for h in heads]
        for h in heads:
            sl = slice(h * d, (h + 1) * d)
            rows = slice(s * c, (s + 1) * c)
            ms = jnp.mean(o[h] * o[h], axis=-1, keepdims=True)
            y = o[h] * lax.rsqrt(ms + RMS_EPS) * nw_ref[...]
            o_ref[rows, sl] = (y * _silu(z_ref[rows, sl].astype(F32))).astype(o_ref.dtype)
    for h in heads:
        st_ref[h] = state[h]

    buf_ref[0:HALO, :] = buf_ref[tt:tt + HALO, :]


def _gdn(proj, small, conv_w, a_log, dt_bias, gdn_norm_w, b, t, nsub=2):
    c = GDN_C
    tt = nsub * c
    nt = t // tt
    qkv_w = 3 * GDN_W
    pad = SMALL_W - GDN_HEADS
    al_row = jnp.pad(a_log.astype(F32), (0, pad)).reshape(1, SMALL_W)
    dtb_row = jnp.pad(dt_bias.astype(F32), (0, pad)).reshape(1, SMALL_W)
    tril = jnp.asarray(np.tril(np.ones((c, c), np.float32)), BF16)
    return pl.pallas_call(
        functools.partial(_gdn_kernel, nsub=nsub),
        grid=(b, nt),
        in_specs=[
            pl.BlockSpec((tt, qkv_w), lambda bi, ti: (bi * nt + ti, GDN_OFF // qkv_w)),
            pl.BlockSpec((tt, GDN_W), lambda bi, ti: (bi * nt + ti, (GDN_OFF + qkv_w) // GDN_W)),
            pl.BlockSpec((tt, SMALL_W), lambda bi, ti: (bi * nt + ti, 0)),
            pl.BlockSpec((GDN_CONV, qkv_w), lambda bi, ti: (0, 0)),
            pl.BlockSpec((1, SMALL_W), lambda bi, ti: (0, 0)),
            pl.BlockSpec((1, SMALL_W), lambda bi, ti: (0, 0)),
            pl.BlockSpec((1, HEAD_DIM), lambda bi, ti: (0, 0)),
            pl.BlockSpec((c, c), lambda bi, ti: (0, 0)),
        ],
        out_specs=pl.BlockSpec((tt, GDN_W), lambda bi, ti: (bi * nt + ti, 0)),
        out_shape=jax.ShapeDtypeStruct((b * t, GDN_W), BF16),
        scratch_shapes=[pltpu.VMEM((HALO + tt, qkv_w), F32),
                        pltpu.VMEM((GDN_HEADS, HEAD_DIM, HEAD_DIM), F32)],
        compiler_params=pltpu.CompilerParams(
            dimension_semantics=("parallel", "arbitrary"), vmem_limit_bytes=VMEM_LIMIT),
        name="gdn",
    )(proj, proj, small, conv_w, al_row, dtb_row, gdn_norm_w.reshape(1, HEAD_DIM), tril)


def _moba_key_features(t):
    pos = np.arange(t)
    blk = pos // MOBA_BLOCK
    off = pos % MOBA_BLOCK
    f = np.zeros((t, HEAD_DIM), np.float32)
    f[pos, blk] = 1.0
    f[:, SEL_LANES] = 1.0
    for p in range(3):
        f[:, SEL_LANES + 1 + p] = blk
        f[:, SEL_LANES + 4 + p] = off // 16
        f[:, SEL_LANES + 7 + p] = off % 16
    return jnp.asarray(f, BF16)


def _moba_query_features():
    hh = MOBA_HEADS
    slopes = np.exp2(-8.0 * np.arange(1, hh + 1, dtype=np.float64) / hh).astype(np.float32)
    s = jnp.asarray(slopes, F32)
    s1 = s.astype(BF16).astype(F32)
    s2 = (s - s1).astype(BF16).astype(F32)
    s3 = (s - s1 - s2).astype(BF16).astype(F32)
    f = jnp.zeros((hh, HEAD_DIM), F32)
    f = f.at[:, SEL_LANES].set(-s * MOBA_BLOCK)
    for p, sp in enumerate((s1, s2, s3)):
        f = f.at[:, SEL_LANES + 1 + p].set(sp * MOBA_BLOCK)
        f = f.at[:, SEL_LANES + 4 + p].set(sp * 16.0)
        f = f.at[:, SEL_LANES + 7 + p].set(sp)
    return f


def _moba_prep_kernel(k_ref, v_ref, nw_ref, pf_ref, ka_ref, va_ref, km_ref):
    ones = jnp.ones((k_ref.shape[0], HEAD_DIM), BF16)
    for h in range(MOBA_HEADS):
        sl = slice(h * HEAD_DIM, (h + 1) * HEAD_DIM)
        k = k_ref[:, sl].astype(F32)
        ms = jnp.mean(k * k, axis=-1, keepdims=True)
        kn = k * lax.rsqrt(ms + RMS_EPS) * nw_ref[...]
        ka_ref[0, h, :, 0:HEAD_DIM] = kn.astype(BF16)
        ka_ref[0, h, :, HEAD_DIM:2 * HEAD_DIM] = pf_ref[...]
        km_ref[0, 0, :, sl] = jnp.mean(kn, axis=0, keepdims=True)
        va_ref[0, h, :, 0:HEAD_DIM] = v_ref[:, sl]
        va_ref[0, h, :, HEAD_DIM:2 * HEAD_DIM] = ones


def _moba_prep(proj, k_norm_w, key_feat, b, t):
    blk = MOBA_BLOCK
    nb = t // blk
    kc = (MOBA_OFF + MOBA_W) // MOBA_W
    aug = pl.BlockSpec((1, MOBA_HEADS, blk, 2 * HEAD_DIM), lambda bi, ti: (bi, 0, ti, 0))
    aug_shape = jax.ShapeDtypeStruct((b, MOBA_HEADS, t, 2 * HEAD_DIM), BF16)
    return pl.pallas_call(
        _moba_prep_kernel,
        grid=(b, nb),
        in_specs=[
            pl.BlockSpec((blk, MOBA_W), lambda bi, ti: (bi * nb + ti, kc)),
            pl.BlockSpec((blk, MOBA_W), lambda bi, ti: (bi * nb + ti, kc + 1)),
            pl.BlockSpec((1, HEAD_DIM), lambda bi, ti: (0, 0)),
            pl.BlockSpec((blk, HEAD_DIM), lambda bi, ti: (ti, 0)),
        ],
        out_specs=[aug, aug, pl.BlockSpec((1, 1, 1, MOBA_W), lambda bi, ti: (bi, ti, 0, 0))],
        out_shape=[aug_shape, aug_shape, jax.ShapeDtypeStruct((b, nb, 1, MOBA_W), F32)],
        compiler_params=pltpu.CompilerParams(
            dimension_semantics=("parallel", "parallel"), vmem_limit_bytes=VMEM_LIMIT),
        name="moba_prep",
    )(proj, proj, k_norm_w.reshape(1, HEAD_DIM), key_feat)


def _moba_qprep_kernel(q_ref, km_ref, qnw_ref, qf_ref, qa_ref):
    tq = q_ref.shape[0]
    nbp = km_ref.shape[1]
    t0 = pl.program_id(1) * tq
    lane = lax.broadcasted_iota(jnp.int32, (tq, HEAD_DIM), 1)
    rowi = lax.broadcasted_iota(jnp.int32, (tq, HEAD_DIM), 0)
    own = (t0 + rowi) // MOBA_BLOCK
    own_coef = jnp.where(lane == SEL_LANES, own.astype(F32), 1.0)
    blk_t = lax.broadcasted_iota(jnp.int32, (nbp, tq), 0)
    blk_f = blk_t.astype(F32)
    own_t = (t0 + lax.broadcasted_iota(jnp.int32, (nbp, tq), 1)) // MOBA_BLOCK
    past = blk_t < own_t
    heads = range(MOBA_HEADS)
    hsl = [slice(h * HEAD_DIM, (h + 1) * HEAD_DIM) for h in heads]

    def qnorm(h):
        q = q_ref[:, hsl[h]].astype(F32)
        ms = jnp.mean(q * q, axis=-1, keepdims=True)
        return q * lax.rsqrt(ms + RMS_EPS) * qnw_ref[...]

    qn = [qnorm(h) for h in heads]
    for h in heads:
        qa_ref[0, h, :, 0:HEAD_DIM] = (qn[h] * SCALE).astype(BF16)

    def gate_scores(h):
        km = km_ref[0, :, hsl[h]]
        q1 = qn[h].astype(BF16)
        q2 = (qn[h] - q1.astype(F32)).astype(BF16)
        k1 = km.astype(BF16)
        k2 = (km - k1.astype(F32)).astype(BF16)
        return jnp.where(past, _dot_nt(k1, q1) + _dot_nt(k2, q1) + _dot_nt(k1, q2), -jnp.inf)

    gate = [gate_scores(h) for h in heads]
    taken = [jnp.zeros((nbp, tq), F32) for _ in heads]
    no_block = float(nbp)
    for _ in range(MOBA_TOP_K):
        gm = [jnp.where(taken[h] > 0.0, -jnp.inf, gate[h]) for h in heads]
        mx = [jnp.max(gm[h], axis=0, keepdims=True) for h in heads]
        cand = [jnp.where(taken[h] > 0.0, no_block,
                          jnp.where(gm[h] == mx[h], blk_f, no_block)) for h in heads]
        idx = [jnp.min(cand[h], axis=0, keepdims=True) for h in heads]
        taken = [jnp.where(blk_f == idx[h], 1.0, taken[h]) for h in heads]
    zero_rows = jnp.zeros((HEAD_DIM - nbp, tq), F32)
    for h in heads:
        sel_t = jnp.where(blk_t == own_t, 0.0,
                          jnp.where(past, jnp.where(taken[h] > 0.0, 0.0, NEG_BIG), NEG_BIG))
        sel_bias = jnp.concatenate([sel_t, zero_rows], axis=0).T
        extra = jnp.where(lane < SEL_LANES, sel_bias, qf_ref[h:h + 1, :] * own_coef)
        qa_ref[0, h, :, HEAD_DIM:2 * HEAD_DIM] = extra.astype(BF16)


def _moba_qprep(proj, k_mean, q_norm_w, q_feat, b, t, tq=256):
    nq = t // tq
    nbp = k_mean.shape[1]
    return pl.pallas_call(
        _moba_qprep_kernel,
        grid=(b, nq),
        in_specs=[
            pl.BlockSpec((tq, MOBA_W), lambda bi, ti: (bi * nq + ti, MOBA_OFF // MOBA_W)),
            pl.BlockSpec((1, nbp, MOBA_W), lambda bi, ti: (bi, 0, 0)),
            pl.BlockSpec((1, HEAD_DIM), lambda bi, ti: (0, 0)),
            pl.BlockSpec((MOBA_HEADS, HEAD_DIM), lambda bi, ti: (0, 0)),
        ],
        out_specs=pl.BlockSpec((1, MOBA_HEADS, tq, 2 * HEAD_DIM), lambda bi, ti: (bi, 0, ti, 0)),
        out_shape=jax.ShapeDtypeStruct((b, MOBA_HEADS, t, 2 * HEAD_DIM), BF16),
        compiler_params=pltpu.CompilerParams(
            dimension_semantics=("parallel", "parallel"), vmem_limit_bytes=VMEM_LIMIT),
        name="moba_qprep",
    )(proj, k_mean, q_norm_w.reshape(1, HEAD_DIM), q_feat)


def _moba_attn_kernel(qa_ref, ka_ref, va_ref, z_ref, o_ref, s_ref, m_ref, acc_ref, *, group, hp, tq):
    blk = MOBA_BLOCK
    d = HEAD_DIM
    gw = group * blk
    qi = pl.program_id(2)
    n_full = (qi * tq) // gw
    heads = range(hp)

    m_ref[...] = jnp.full(m_ref.shape, NEG_BIG, F32)
    acc_ref[...] = jnp.zeros(acc_ref.shape, F32)

    def store_scores(slot, g):
        off = pl.multiple_of(g * gw, gw)
        for h in heads:
            s_ref[slot, h] = _dot_nt(qa_ref[0, h], ka_ref[0, h, pl.ds(off, gw), :])

    def update(s, g):
        off = pl.multiple_of(g * gw, gw)
        m_prev = [m_ref[h] for h in heads]
        m_new = [jnp.maximum(m_prev[h], jnp.max(s[h], axis=-1, keepdims=True)) for h in heads]
        alpha = [jnp.exp(m_prev[h] - m_new[h]) for h in heads]
        p = [jnp.exp(s[h] - jnp.concatenate([m_new[h]] * (gw // d), axis=-1)).astype(BF16)
             for h in heads]
        for h in heads:
            m_ref[h] = m_new[h]
        pv = [_dot(p[h], va_ref[0, h, pl.ds(off, gw), :]) for h in heads]
        for h in heads:
            acc_ref[h] = jnp.concatenate([alpha[h], alpha[h]], axis=-1) * acc_ref[h] + pv[h]

    store_scores(0, 0)

    def body(g, carry):
        slot = g % 2
        s = [s_ref[slot, h] for h in heads]
        update(s, g)
        store_scores(1 - slot, g + 1)
        return carry

    lax.fori_loop(0, n_full, body, 0)

    qpos = qi * tq + lax.broadcasted_iota(jnp.int32, (tq, gw), 0)
    kpos = n_full * gw + lax.broadcasted_iota(jnp.int32, (tq, gw), 1)
    visible = kpos <= qpos
    slot = n_full % 2
    update([jnp.where(visible, s_ref[slot, h], NEG_BIG) for h in heads], n_full)

    for h in heads:
        sl = slice(h * d, (h + 1) * d)
        o = acc_ref[h, :, 0:d] / acc_ref[h, :, d:2 * d]
        o_ref[:, sl] = (o * _silu(z_ref[:, sl].astype(F32))).astype(o_ref.dtype)


def _moba_attn(proj, q_aug, k_aug, v_aug, b, t, group=4, hp=2, tq=512):
    blk = MOBA_BLOCK
    nb = t // blk
    nq = t // tq
    w = hp * HEAD_DIM
    zc = (MOBA_OFF + 3 * MOBA_W) // w
    assert nb % group == 0 and MOBA_HEADS % hp == 0 and MOBA_W % w == 0 and MOBA_OFF % w == 0
    assert tq % blk == 0 and (group * blk) % tq == 0
    resident = pl.BlockSpec((1, hp, t, 2 * HEAD_DIM), lambda bi, hi, qi: (bi, hi, 0, 0))
    return pl.pallas_call(
        functools.partial(_moba_attn_kernel, group=group, hp=hp, tq=tq),
        grid=(b, MOBA_HEADS // hp, nq),
        in_specs=[
            pl.BlockSpec((1, hp, tq, 2 * HEAD_DIM), lambda bi, hi, qi: (bi, hi, qi, 0)),
            resident,
            resident,
            pl.BlockSpec((tq, w), lambda bi, hi, qi: (bi * nq + qi, zc + hi)),
        ],
        out_specs=pl.BlockSpec((tq, w), lambda bi, hi, qi: (bi * nq + qi, hi)),
        out_shape=jax.ShapeDtypeStruct((b * t, MOBA_W), BF16),
        scratch_shapes=[pltpu.VMEM((2, hp, tq, group * blk), F32),
                        pltpu.VMEM((hp, tq, HEAD_DIM), F32),
                        pltpu.VMEM((hp, tq, 2 * HEAD_DIM), F32)],
        compiler_params=pltpu.CompilerParams(
            dimension_semantics=("parallel", "parallel", "arbitrary"), vmem_limit_bytes=VMEM_LIMIT),
        name="moba_attn",
    )(q_aug, k_aug, v_aug, proj)


def _layer(x2d, h2d, b, t, layer, w_a, w_b, w_small, conv_w, a_log, dt_bias, ret_norm_w, gdn_norm_w,
           q_norm_w, k_norm_w, w_out, next_norm_w, key_feat, q_feat):
    nb = t // MOBA_BLOCK
    proj, small = _in_proj(h2d, w_a, w_b, w_small, layer)
    y_ret = _retention(proj, ret_norm_w, b, t)
    y_gdn = _gdn(proj, small, conv_w, a_log, dt_bias, gdn_norm_w, b, t)
    k_aug, v_aug, k_mean = _moba_prep(proj, k_norm_w, key_feat, b, t)
    k_mean = jnp.pad(k_mean.reshape(b, nb, MOBA_W), ((0, 0), (0, -nb % 8), (0, 0)))
    q_aug = _moba_qprep(proj, k_mean, q_norm_w, q_feat, b, t)
    y_moba = _moba_attn(proj, q_aug, k_aug, v_aug, b, t)
    return _out_proj(x2d, y_ret, y_gdn, y_moba, w_out, layer, next_norm_w)


def kernel(x, norm_w, w_in, conv_w, a_log, dt_bias, ret_norm_w, gdn_norm_w, q_norm_w, k_norm_w, w_out):
    b, t, d = x.shape
    depth = w_in.shape[0]
    assert t % MOBA_BLOCK == 0 and t // MOBA_BLOCK <= SEL_LANES
    small_lo = 4 * RET_W + 4 * GDN_W
    moba_lo = small_lo + 2 * GDN_HEADS
    w_t = jnp.swapaxes(w_in, 1, 2)
    w_a = w_t[:, :small_lo, :].astype(BF16)
    w_b = w_t[:, moba_lo:, :].astype(BF16)
    w_small = jnp.pad(w_t[:, small_lo:moba_lo, :],
                      ((0, 0), (0, SMALL_W - 2 * GDN_HEADS), (0, 0))).astype(BF16)
    w_out_b = w_out.astype(BF16)
    key_feat = _moba_key_features(t)
    q_feat = _moba_query_features()
    x2d = x.reshape(b * t, d)
    h2d = _norm(x2d, norm_w[0])
    for layer in range(depth):
        next_norm_w = norm_w[layer + 1] if layer + 1 < depth else None
        x2d, h2d = _layer(x2d, h2d, b, t, layer, w_a, w_b, w_small, conv_w[layer], a_log[layer],
                          dt_bias[layer], ret_norm_w[layer], gdn_norm_w[layer], q_norm_w[layer],
                          k_norm_w[layer], w_out_b, next_norm_w, key_feat, q_feat)
    return x2d.reshape(b, t, d)
```

```python
import functools

import numpy as np
import jax
import jax.numpy as jnp
from jax import lax
from jax.experimental import pallas as pl
from jax.experimental.pallas import tpu as pltpu

F32 = jnp.float32
BF16 = jnp.bfloat16

HEAD_DIM = 128
RET_HEADS = 4
GDN_HEADS = 6
MOBA_HEADS = 6
RET_W = RET_HEADS * HEAD_DIM
GDN_W = GDN_HEADS * HEAD_DIM
MOBA_W = MOBA_HEADS * HEAD_DIM
GDN_CONV = 4
MOBA_BLOCK = 256
MOBA_TOP_K = 3
RMS_EPS = 1e-6
SCALE = HEAD_DIM ** -0.5

GDN_OFF = 0
MOBA_OFF = 4 * GDN_W
RET_OFF = 4 * GDN_W + 4 * MOBA_W
MAIN_W = RET_OFF + 4 * RET_W
SMALL_W = 128

RET_C = 256
GDN_C = 128
HALO = 8

NEG_BIG = -1e30
SEL_LANES = 32
VMEM_LIMIT = 56 * 1024 * 1024


def _dot(a, b):
    return jnp.dot(a, b, preferred_element_type=F32)


def _dot_nt(a, b):
    return lax.dot_general(a, b, (((1,), (1,)), ((), ())), preferred_element_type=F32)


def _dot_tn(a, b):
    return lax.dot_general(a, b, (((0,), (0,)), ((), ())), preferred_element_type=F32)


def _sigmoid(x):
    return 1.0 / (1.0 + jnp.exp(-x))


def _silu(x):
    return x * _sigmoid(x)


def _split3(x):
    x1 = x.astype(BF16)
    r = x - x1.astype(F32)
    x2 = r.astype(BF16)
    x3 = (r - x2.astype(F32)).astype(BF16)
    return x1, x2, x3


def _rmsnorm_rows(x, nw):
    ms = jnp.mean(x * x, axis=-1, keepdims=True)
    return (x * lax.rsqrt(ms + RMS_EPS) * nw).astype(BF16)


def _norm_kernel(x_ref, nw_ref, h_ref):
    h_ref[...] = _rmsnorm_rows(x_ref[...], nw_ref[...])


def _norm(x2d, norm_w, tm=512):
    m, d = x2d.shape
    return pl.pallas_call(
        _norm_kernel,
        grid=(m // tm,),
        in_specs=[pl.BlockSpec((tm, d), lambda i: (i, 0)), pl.BlockSpec((1, d), lambda i: (0, 0))],
        out_specs=pl.BlockSpec((tm, d), lambda i: (i, 0)),
        out_shape=jax.ShapeDtypeStruct((m, d), BF16),
        compiler_params=pltpu.CompilerParams(
            dimension_semantics=("parallel",), vmem_limit_bytes=VMEM_LIMIT),
        name="rmsnorm",
    )(x2d, norm_w.reshape(1, d))


def _in_proj_kernel(h_ref, wa_ref, wb_ref, ws_ref, o_ref, os_ref, *, na):
    j = pl.program_id(1)

    @pl.when(j == 0)
    def _():
        os_ref[...] = _dot_nt(h_ref[...], ws_ref[0])

    @pl.when(j < na)
    def _():
        o_ref[...] = _dot_nt(h_ref[...], wa_ref[0]).astype(o_ref.dtype)

    @pl.when(j >= na)
    def _():
        o_ref[...] = _dot_nt(h_ref[...], wb_ref[0]).astype(o_ref.dtype)


def _in_proj(h2d, w_a, w_b, layer, tm=2048, tn=512):
    m, d = h2d.shape
    nr = 4 * RET_W // tn
    na = (4 * RET_W + 4 * GDN_W) // tn
    nb = 4 * MOBA_W // tn
    small_blk = na * tn // SMALL_W
    assert 4 * RET_W % tn == 0 and 4 * GDN_W % tn == 0 and 4 * MOBA_W % tn == 0 and m % tm == 0
    assert (na * tn) % SMALL_W == 0 and na * tn + SMALL_W <= w_a.shape[1]
    return pl.pallas_call(
        functools.partial(_in_proj_kernel, na=na),
        grid=(m // tm, na + nb),
        in_specs=[
            pl.BlockSpec((tm, d), lambda i, j: (i, 0)),
            pl.BlockSpec((1, tn, d), lambda i, j: (layer, jnp.minimum(j, na - 1), 0)),
            pl.BlockSpec((1, tn, d), lambda i, j: (layer, jnp.maximum(j - na, 0), 0)),
            pl.BlockSpec((1, SMALL_W, d), lambda i, j: (layer, small_blk, 0)),
        ],
        out_specs=[
            pl.BlockSpec((tm, tn), lambda i, j: (i, jnp.where(j < nr, j + (na - nr) + nb, j - nr))),
            pl.BlockSpec((tm, SMALL_W), lambda i, j: (i, 0)),
        ],
        out_shape=[
            jax.ShapeDtypeStruct((m, MAIN_W), BF16),
            jax.ShapeDtypeStruct((m, SMALL_W), F32),
        ],
        compiler_params=pltpu.CompilerParams(
            dimension_semantics=("parallel", "arbitrary"), vmem_limit_bytes=VMEM_LIMIT),
        name="in_proj",
    )(h2d, w_a, w_b, w_a)


def _out_proj_kernel(x_ref, yr_ref, yg_ref, ym_ref, w_ref, *rest, tn, with_norm):
    if with_norm:
        nw_ref, o_ref, h_ref = rest
    else:
        (o_ref,) = rest
    n = o_ref.shape[1]
    yr = yr_ref[...]
    yg = yg_ref[...]
    ym = ym_ref[...]
    for c in range(0, n, tn):
        acc = x_ref[:, c:c + tn]
        acc = acc + _dot(yr, w_ref[0, 0:RET_W, c:c + tn])
        acc = acc + _dot(yg, w_ref[0, RET_W:RET_W + GDN_W, c:c + tn])
        acc = acc + _dot(ym, w_ref[0, RET_W + GDN_W:RET_W + GDN_W + MOBA_W, c:c + tn])
        o_ref[:, c:c + tn] = acc
    if with_norm:
        h_ref[...] = _rmsnorm_rows(o_ref[...], nw_ref[...])


def _out_proj(x2d, y_ret, y_gdn, y_moba, w_out, layer, next_norm_w=None, tm=512, tn=512):
    m, d = x2d.shape
    with_norm = next_norm_w is not None
    in_specs = [
        pl.BlockSpec((tm, d), lambda i: (i, 0)),
        pl.BlockSpec((tm, RET_W), lambda i: (i, 0)),
        pl.BlockSpec((tm, GDN_W), lambda i: (i, 0)),
        pl.BlockSpec((tm, MOBA_W), lambda i: (i, 0)),
        pl.BlockSpec((1,) + w_out.shape[1:], lambda i: (layer, 0, 0)),
    ]
    args = [x2d, y_ret, y_gdn, y_moba, w_out]
    out_specs = [pl.BlockSpec((tm, d), lambda i: (i, 0))]
    out_shape = [jax.ShapeDtypeStruct((m, d), F32)]
    if with_norm:
        in_specs.append(pl.BlockSpec((1, d), lambda i: (0, 0)))
        args.append(next_norm_w.reshape(1, d))
        out_specs.append(pl.BlockSpec((tm, d), lambda i: (i, 0)))
        out_shape.append(jax.ShapeDtypeStruct((m, d), BF16))
    res = pl.pallas_call(
        functools.partial(_out_proj_kernel, tn=tn, with_norm=with_norm),
        grid=(m // tm,),
        in_specs=in_specs,
        out_specs=out_specs,
        out_shape=out_shape,
        compiler_params=pltpu.CompilerParams(
            dimension_semantics=("parallel",), vmem_limit_bytes=VMEM_LIMIT),
        name="out_proj",
    )(*args)
    return (res[0], res[1]) if with_norm else (res[0], None)


def _retention_consts(c):
    h = np.arange(RET_HEADS, dtype=np.float64)
    log_gamma = np.log1p(-np.exp2(-5.0 - h))
    pos = np.arange(c, dtype=np.float64)
    diff = pos[:, None] - pos[None, :]
    dmat = np.where(diff >= 0, np.exp(np.maximum(diff, 0.0) * log_gamma[:, None, None]), 0.0) * SCALE
    kws = np.exp((c - 1 - pos)[None, :] * log_gamma[:, None])
    qws = np.exp((pos + 1.0)[None, :] * log_gamma[:, None]) * SCALE
    kws = np.broadcast_to(kws[:, :, None], (RET_HEADS, c, HEAD_DIM))
    qws = np.broadcast_to(qws[:, :, None], (RET_HEADS, c, HEAD_DIM))
    chunk_decay = [float(v) for v in np.exp(c * log_gamma)]
    return (jnp.asarray(dmat, F32), jnp.asarray(kws, F32), jnp.asarray(qws, F32), chunk_decay)


def _retention_kernel(q_ref, k_ref, v_ref, g_ref, dm_ref, kws_ref, qws_ref, nw_ref, o_ref, st_ref,
                      *, chunk_decay):
    @pl.when(pl.program_id(1) == 0)
    def _():
        st_ref[...] = jnp.zeros_like(st_ref)

    heads = range(RET_HEADS)
    hsl = [slice(h * HEAD_DIM, (h + 1) * HEAD_DIM) for h in heads]
    q = [q_ref[:, hsl[h]] for h in heads]
    k = [k_ref[:, hsl[h]] for h in heads]
    v = [v_ref[:, hsl[h]] for h in heads]
    s_b = [(_dot_nt(q[h], k[h]) * dm_ref[h]).astype(BF16) for h in heads]
    state = [st_ref[h] for h in heads]
    qw = [(q[h].astype(F32) * qws_ref[h]).astype(BF16) for h in heads]
    kw = [(k[h].astype(F32) * kws_ref[h]).astype(BF16) for h in heads]
    o = [_dot(s_b[h], v[h]) + _dot(qw[h], state[h].astype(BF16)) for h in heads]
    for h in heads:
        st_ref[h] = state[h] * chunk_decay[h] + _dot_tn(kw[h], v[h])
    for h in heads:
        ms = jnp.mean(o[h] * o[h], axis=-1, keepdims=True)
        y = o[h] * lax.rsqrt(ms + RMS_EPS) * nw_ref[h:h + 1, :]
        o_ref[:, hsl[h]] = (y * _silu(g_ref[:, hsl[h]].astype(F32))).astype(o_ref.dtype)


def _retention(proj, ret_norm_w, b, t):
    c = RET_C
    nt = t // c
    dmat, kws, qws, chunk_decay = _retention_consts(c)
    base = RET_OFF // RET_W

    def col(ci):
        return pl.BlockSpec((c, RET_W), lambda bi, ti: (bi * nt + ti, base + ci))

    def const3(shape):
        return pl.BlockSpec(shape, lambda bi, ti: (0, 0, 0))

    return pl.pallas_call(
        functools.partial(_retention_kernel, chunk_decay=chunk_decay),
        grid=(b, nt),
        in_specs=[col(0), col(1), col(2), col(3),
                  const3(dmat.shape), const3(kws.shape), const3(qws.shape),
                  pl.BlockSpec((RET_HEADS, HEAD_DIM), lambda bi, ti: (0, 0))],
        out_specs=pl.BlockSpec((c, RET_W), lambda bi, ti: (bi * nt + ti, 0)),
        out_shape=jax.ShapeDtypeStruct((b * t, RET_W), BF16),
        scratch_shapes=[pltpu.VMEM((RET_HEADS, HEAD_DIM, HEAD_DIM), F32)],
        compiler_params=pltpu.CompilerParams(
            dimension_semantics=("parallel", "arbitrary"), vmem_limit_bytes=VMEM_LIMIT),
        name="retention",
    )(proj, proj, proj, proj, dmat, kws, qws, ret_norm_w)


def _gdn_kernel(qkv_ref, z_ref, sm_ref, cw_ref, al_ref, dtb_ref, nw_ref, tril_ref, o_ref,
                buf_ref, st_ref, *, nsub):
    c = GDN_C
    d = HEAD_DIM
    tt = nsub * c
    heads = range(GDN_HEADS)

    @pl.when(pl.program_id(1) == 0)
    def _():
        buf_ref[0:HALO, :] = jnp.zeros((HALO, buf_ref.shape[1]), F32)
        st_ref[...] = jnp.zeros_like(st_ref)

    buf_ref[HALO:HALO + tt, :] = qkv_ref[...].astype(F32)

    def conv_act(s, col):
        sl = slice(col * d, (col + 1) * d)
        r0 = HALO + s * c
        acc = cw_ref[GDN_CONV - 1:GDN_CONV, sl] * buf_ref[r0:r0 + c, sl]
        for j in range(GDN_CONV - 1):
            sh = GDN_CONV - 1 - j
            acc = acc + cw_ref[j:j + 1, sl] * buf_ref[r0 - sh:r0 - sh + c, sl]
        return _silu(acc)

    def l2n(t):
        return t * lax.rsqrt(jnp.sum(t * t, axis=-1, keepdims=True) + RMS_EPS)

    sm = sm_ref[...]
    x = sm + dtb_ref[...]
    softplus = jnp.maximum(x, 0.0) + jnp.log1p(jnp.exp(-jnp.abs(x)))
    g = -jnp.exp(al_ref[...]) * softplus
    beta = _sigmoid(sm)
    tril = tril_ref[...]
    gc, gct = [], []
    for s in range(nsub):
        g1, g2, g3 = _split3(g[s * c:(s + 1) * c])
        gcs = _dot(tril, g1) + _dot(tril, g2) + _dot(tril, g3)
        gc.append(gcs)
        gct.append(gcs.T)

    row = lax.broadcasted_iota(jnp.int32, (c, c), 0)
    coli = lax.broadcasted_iota(jnp.int32, (c, c), 1)
    causal = row >= coli
    strict = row > coli
    eye = jnp.where(row == coli, 1.0, 0.0).astype(F32)
    diag8 = strict & ((row >> 3) == (coli >> 3))
    merge_masks = [((row >> (lv + 1)) == (coli >> (lv + 1))) & ((row >> lv) != (coli >> lv)) & strict
                   for lv in range(3, 7)]

    units = [(s, h) for s in range(nsub) for h in heads]
    un = range(len(units))
    q = [l2n(conv_act(s, h)) * SCALE for s, h in units]
    k = [l2n(conv_act(s, GDN_HEADS + h)) for s, h in units]
    v = [conv_act(s, 2 * GDN_HEADS + h) for s, h in units]
    gcol = [jnp.broadcast_to(gc[s][:, h:h + 1], (c, c)) for s, h in units]
    dm = [jnp.exp(jnp.where(causal, gcol[u] - gct[s][h:h + 1, :], NEG_BIG))
          for u, (s, h) in enumerate(units)]
    bb = [jnp.broadcast_to(beta[s * c:(s + 1) * c, GDN_HEADS + h:GDN_HEADS + h + 1], (c, d))
          for s, h in units]
    eg = [jnp.exp(gcol[u]) for u in un]
    glast = [gcol[u][c - 1:c, :] for u in un]
    ek = [jnp.exp(glast[u] - gcol[u]) for u in un]
    kb = [k[u] * bb[u] for u in un]
    k_b = [k[u].astype(BF16) for u in un]

    sc = [_dot_nt(jnp.concatenate([kb[u], q[u]], axis=0).astype(BF16), k_b[u]) for u in un]
    mm = [jnp.where(strict, sc[u][:c] * dm[u], 0.0) for u in un]
    qk_b = [(sc[u][c:] * dm[u]).astype(BF16) for u in un]

    md = [jnp.where(diag8, mm[u], 0.0) for u in un]
    md_b = [md[u].astype(BF16) for u in un]
    x0 = [eye - md[u] for u in un]
    md2_b = [_dot(md_b[u], md_b[u]).astype(BF16) for u in un]
    t1 = [_dot(md2_b[u], jnp.concatenate([x0[u].astype(BF16), md2_b[u]], axis=-1)) for u in un]
    x1 = [x0[u] + t1[u][:, :c] for u in un]
    md4_b = [t1[u][:, c:].astype(BF16) for u in un]
    xi = [x1[u] + _dot(md4_b[u], x1[u].astype(BF16)) for u in un]
    for mask in merge_masks:
        mo_b = [jnp.where(mask, mm[u], 0.0).astype(BF16) for u in un]
        xi_b = [xi[u].astype(BF16) for u in un]
        y_b = [_dot(xi_b[u], mo_b[u]).astype(BF16) for u in un]
        xi = [xi[u] - _dot(y_b[u], xi_b[u]) for u in un]
    xc_b = [(xi[u] - eye).astype(BF16) for u in un]

    rhs = [jnp.concatenate([kb[u] * eg[u], v[u] * bb[u]], axis=-1) for u in un]
    wu = [rhs[u] + _dot(xc_b[u], rhs[u].astype(BF16)) for u in un]
    wq_b = [jnp.concatenate([wu[u][:, :d], q[u] * eg[u]], axis=0).astype(BF16) for u in un]
    kg_b = [(k[u] * ek[u]).astype(BF16) for u in un]
    elast = [jnp.exp(glast[u]) for u in un]

    state = [st_ref[h] for h in heads]
    for s in range(nsub):
        us = [s * GDN_HEADS + h for h in heads]
        state_b = [state[h].astype(BF16) for h in heads]
        ws = [_dot(wq_b[us[h]], state_b[h]) for h in heads]
        v_new_b = [(wu[us[h]][:, d:] - ws[h][:c]).astype(BF16) for h in heads]
        o = [ws[h][c:] + _dot(qk_b[us[h]], v_new_b[h]) for h in heads]
        state = [state[h] * elast[us[h]] + _dot_tn(kg_b[us[h]], v_new_b[h]) for h in heads]
        for h in heads:
            sl = slice(h * d, (h + 1) * d)
            rows = slice(s * c, (s + 1) * c)
            ms = jnp.mean(o[h] * o[h], axis=-1, keepdims=True)
            y = o[h] * lax.rsqrt(ms + RMS_EPS) * nw_ref[...]
            o_ref[rows, sl] = (y * _silu(z_ref[rows, sl].astype(F32))).astype(o_ref.dtype)
    for h in heads:
        st_ref[h] = state[h]

    buf_ref[0:HALO, :] = buf_ref[tt:tt + HALO, :]


def _gdn(proj, small, conv_w, a_log, dt_bias, gdn_norm_w, b, t, nsub=2):
    c = GDN_C
    tt = nsub * c
    nt = t // tt
    qkv_w = 3 * GDN_W
    pad = SMALL_W - GDN_HEADS
    al_row = jnp.pad(a_log.astype(F32), (0, pad)).reshape(1, SMALL_W)
    dtb_row = jnp.pad(dt_bias.astype(F32), (0, pad)).reshape(1, SMALL_W)
    tril = jnp.asarray(np.tril(np.ones((c, c), np.float32)), BF16)
    return pl.pallas_call(
        functools.partial(_gdn_kernel, nsub=nsub),
        grid=(b, nt),
        in_specs=[
            pl.BlockSpec((tt, qkv_w), lambda bi, ti: (bi * nt + ti, GDN_OFF // qkv_w)),
            pl.BlockSpec((tt, GDN_W), lambda bi, ti: (bi * nt + ti, (GDN_OFF + qkv_w) // GDN_W)),
            pl.BlockSpec((tt, SMALL_W), lambda bi, ti: (bi * nt + ti, 0)),
            pl.BlockSpec((GDN_CONV, qkv_w), lambda bi, ti: (0, 0)),
            pl.BlockSpec((1, SMALL_W), lambda bi, ti: (0, 0)),
            pl.BlockSpec((1, SMALL_W), lambda bi, ti: (0, 0)),
            pl.BlockSpec((1, HEAD_DIM), lambda bi, ti: (0, 0)),
            pl.BlockSpec((c, c), lambda bi, ti: (0, 0)),
        ],
        out_specs=pl.BlockSpec((tt, GDN_W), lambda bi, ti: (bi * nt + ti, 0)),
        out_shape=jax.ShapeDtypeStruct((b * t, GDN_W), BF16),
        scratch_shapes=[pltpu.VMEM((HALO + tt, qkv_w), F32),
                        pltpu.VMEM((GDN_HEADS, HEAD_DIM, HEAD_DIM), F32)],
        compiler_params=pltpu.CompilerParams(
            dimension_semantics=("parallel", "arbitrary"), vmem_limit_bytes=VMEM_LIMIT),
        name="gdn",
    )(proj, proj, small, conv_w, al_row, dtb_row, gdn_norm_w.reshape(1, HEAD_DIM), tril)


def _moba_key_features(t):
    pos = np.arange(t)
    blk = pos // MOBA_BLOCK
    off = pos % MOBA_BLOCK
    f = np.zeros((t, HEAD_DIM), np.float32)
    f[pos, blk] = 1.0
    f[:, SEL_LANES] = 1.0
    for p in range(3):
        f[:, SEL_LANES + 1 + p] = blk
        f[:, SEL_LANES + 4 + p] = off // 16
        f[:, SEL_LANES + 7 + p] = off % 16
    return jnp.asarray(f, BF16)


def _moba_query_features():
    hh = MOBA_HEADS
    slopes = np.exp2(-8.0 * np.arange(1, hh + 1, dtype=np.float64) / hh).astype(np.float32)
    s = jnp.asarray(slopes, F32)
    s1 = s.astype(BF16).astype(F32)
    s2 = (s - s1).astype(BF16).astype(F32)
    s3 = (s - s1 - s2).astype(BF16).astype(F32)
    f = jnp.zeros((hh, HEAD_DIM), F32)
    f = f.at[:, SEL_LANES].set(-s * MOBA_BLOCK)
    for p, sp in enumerate((s1, s2, s3)):
        f = f.at[:, SEL_LANES + 1 + p].set(sp * MOBA_BLOCK)
        f = f.at[:, SEL_LANES + 4 + p].set(sp * 16.0)
        f = f.at[:, SEL_LANES + 7 + p].set(sp)
    return f


def _moba_prep_kernel(k_ref, v_ref, nw_ref, pf_ref, ka_ref, va_ref, km_ref):
    ones = jnp.ones((k_ref.shape[0], HEAD_DIM), BF16)
    for h in range(MOBA_HEADS):
        sl = slice(h * HEAD_DIM, (h + 1) * HEAD_DIM)
        k = k_ref[:, sl].astype(F32)
        ms = jnp.mean(k * k, axis=-1, keepdims=True)
        kn = k * lax.rsqrt(ms + RMS_EPS) * nw_ref[...]
        ka_ref[0, h, :, 0:HEAD_DIM] = kn.astype(BF16)
        ka_ref[0, h, :, HEAD_DIM:2 * HEAD_DIM] = pf_ref[...]
        for j in range(k_ref.shape[0] // MOBA_BLOCK):
            rows = slice(j * MOBA_BLOCK, (j + 1) * MOBA_BLOCK)
            km_ref[0, j, :, sl] = jnp.mean(kn[rows], axis=0, keepdims=True)
        va_ref[0, h, :, 0:HEAD_DIM] = v_ref[:, sl]
        va_ref[0, h, :, HEAD_DIM:2 * HEAD_DIM] = ones


def _moba_prep(proj, k_norm_w, key_feat, b, t, bpt=2):
    blk = bpt * MOBA_BLOCK
    nb = t // blk
    kc = (MOBA_OFF + MOBA_W) // MOBA_W
    aug = pl.BlockSpec((1, MOBA_HEADS, blk, 2 * HEAD_DIM), lambda bi, ti: (bi, 0, ti, 0))
    aug_shape = jax.ShapeDtypeStruct((b, MOBA_HEADS, t, 2 * HEAD_DIM), BF16)
    return pl.pallas_call(
        _moba_prep_kernel,
        grid=(b, nb),
        in_specs=[
            pl.BlockSpec((blk, MOBA_W), lambda bi, ti: (bi * nb + ti, kc)),
            pl.BlockSpec((blk, MOBA_W), lambda bi, ti: (bi * nb + ti, kc + 1)),
            pl.BlockSpec((1, HEAD_DIM), lambda bi, ti: (0, 0)),
            pl.BlockSpec((blk, HEAD_DIM), lambda bi, ti: (ti, 0)),
        ],
        out_specs=[aug, aug, pl.BlockSpec((1, bpt, 1, MOBA_W), lambda bi, ti: (bi, ti, 0, 0))],
        out_shape=[aug_shape, aug_shape, jax.ShapeDtypeStruct((b, nb * bpt, 1, MOBA_W), F32)],
        compiler_params=pltpu.CompilerParams(
            dimension_semantics=("parallel", "parallel"), vmem_limit_bytes=VMEM_LIMIT),
        name="moba_prep",
    )(proj, proj, k_norm_w.reshape(1, HEAD_DIM), key_feat)


def _moba_qprep_kernel(q_ref, km_ref, qnw_ref, qf_ref, qa_ref):
    tq = q_ref.shape[0]
    nbp = km_ref.shape[1]
    t0 = pl.program_id(1) * tq
    lane = lax.broadcasted_iota(jnp.int32, (tq, HEAD_DIM), 1)
    rowi = lax.broadcasted_iota(jnp.int32, (tq, HEAD_DIM), 0)
    own = (t0 + rowi) // MOBA_BLOCK
    own_coef = jnp.where(lane == SEL_LANES, own.astype(F32), 1.0)
    blk_t = lax.broadcasted_iota(jnp.int32, (nbp, tq), 0)
    blk_f = blk_t.astype(F32)
    own_t = (t0 + lax.broadcasted_iota(jnp.int32, (nbp, tq), 1)) // MOBA_BLOCK
    past = blk_t < own_t
    heads = range(MOBA_HEADS)
    hsl = [slice(h * HEAD_DIM, (h + 1) * HEAD_DIM) for h in heads]

    def qnorm(h):
        q = q_ref[:, hsl[h]].astype(F32)
        ms = jnp.mean(q * q, axis=-1, keepdims=True)
        return q * lax.rsqrt(ms + RMS_EPS) * qnw_ref[...]

    qn = [qnorm(h) for h in heads]
    for h in heads:
        qa_ref[0, h, :, 0:HEAD_DIM] = (qn[h] * SCALE).astype(BF16)

    def gate_scores(h):
        km = km_ref[0, :, hsl[h]]
        q1 = qn[h].astype(BF16)
        q2 = (qn[h] - q1.astype(F32)).astype(BF16)
        k1 = km.astype(BF16)
        k2 = (km - k1.astype(F32)).astype(BF16)
        return jnp.where(past, _dot_nt(k1, q1) + _dot_nt(k2, q1) + _dot_nt(k1, q2), -jnp.inf)

    gate = [gate_scores(h) for h in heads]
    taken = [jnp.zeros((nbp, tq), F32) for _ in heads]
    no_block = float(nbp)
    for _ in range(MOBA_TOP_K):
        gm = [jnp.where(taken[h] > 0.0, -jnp.inf, gate[h]) for h in heads]
        mx = [jnp.max(gm[h], axis=0, keepdims=True) for h in heads]
        cand = [jnp.where(taken[h] > 0.0, no_block,
                          jnp.where(gm[h] == mx[h], blk_f, no_block)) for h in heads]
        idx = [jnp.min(cand[h], axis=0, keepdims=True) for h in heads]
        taken = [jnp.where(blk_f == idx[h], 1.0, taken[h]) for h in heads]
    zero_rows = jnp.zeros((HEAD_DIM - nbp, tq), F32)
    for h in heads:
        sel_t = jnp.where(blk_t == own_t, 0.0,
                          jnp.where(past, jnp.where(taken[h] > 0.0, 0.0, NEG_BIG), NEG_BIG))
        sel_bias = jnp.concatenate([sel_t, zero_rows], axis=0).T
        extra = jnp.where(lane < SEL_LANES, sel_bias, qf_ref[h:h + 1, :] * own_coef)
        qa_ref[0, h, :, HEAD_DIM:2 * HEAD_DIM] = extra.astype(BF16)


def _moba_qprep(proj, k_mean, q_norm_w, q_feat, b, t, tq=256):
    nq = t // tq
    nbp = k_mean.shape[1]
    return pl.pallas_call(
        _moba_qprep_kernel,
        grid=(b, nq),
        in_specs=[
            pl.BlockSpec((tq, MOBA_W), lambda bi, ti: (bi * nq + ti, MOBA_OFF // MOBA_W)),
            pl.BlockSpec((1, nbp, MOBA_W), lambda bi, ti: (bi, 0, 0)),
            pl.BlockSpec((1, HEAD_DIM), lambda bi, ti: (0, 0)),
            pl.BlockSpec((MOBA_HEADS, HEAD_DIM), lambda bi, ti: (0, 0)),
        ],
        out_specs=pl.BlockSpec((1, MOBA_HEADS, tq, 2 * HEAD_DIM), lambda bi, ti: (bi, 0, ti, 0)),
        out_shape=jax.ShapeDtypeStruct((b, MOBA_HEADS, t, 2 * HEAD_DIM), BF16),
        compiler_params=pltpu.CompilerParams(
            dimension_semantics=("parallel", "parallel"), vmem_limit_bytes=VMEM_LIMIT),
        name="moba_qprep",
    )(proj, k_mean, q_norm_w.reshape(1, HEAD_DIM), q_feat)


def _moba_attn_kernel(qa_ref, ka_ref, va_ref, z_ref, o_ref, s_ref, m_ref, acc_ref, *, group, hp, tq):
    blk = MOBA_BLOCK
    d = HEAD_DIM
    gw = group * blk
    qi = pl.program_id(2)
    n_full = (qi * tq) // gw
    heads = range(hp)

    m_ref[...] = jnp.full(m_ref.shape, NEG_BIG, F32)
    acc_ref[...] = jnp.zeros(acc_ref.shape, F32)

    def store_scores(slot, g):
        off = pl.multiple_of(g * gw, gw)
        for h in heads:
            s_ref[slot, h] = _dot_nt(qa_ref[0, h], ka_ref[0, h, pl.ds(off, gw), :])

    def update(s, g):
        off = pl.multiple_of(g * gw, gw)
        m_prev = [m_ref[h] for h in heads]
        m_new = [jnp.maximum(m_prev[h], jnp.max(s[h], axis=-1, keepdims=True)) for h in heads]
        alpha = [jnp.exp(m_prev[h] - m_new[h]) for h in heads]
        p = [jnp.exp(s[h] - jnp.concatenate([m_new[h]] * (gw // d), axis=-1)).astype(BF16)
             for h in heads]
        for h in heads:
            m_ref[h] = m_new[h]
        pv = [_dot(p[h], va_ref[0, h, pl.ds(off, gw), :]) for h in heads]
        for h in heads:
            acc_ref[h] = jnp.concatenate([alpha[h], alpha[h]], axis=-1) * acc_ref[h] + pv[h]

    def step(slot, g):
        s = [s_ref[slot, h] for h in heads]
        update(s, g)
        store_scores(1 - slot, g + 1)

    store_scores(0, 0)

    def body(i, carry):
        step(0, 2 * i)
        step(1, 2 * i + 1)
        return carry

    lax.fori_loop(0, n_full // 2, body, 0)

    @pl.when(n_full % 2 == 1)
    def _():
        step(0, n_full - 1)

    qpos = qi * tq + lax.broadcasted_iota(jnp.int32, (tq, gw), 0)
    kpos = n_full * gw + lax.broadcasted_iota(jnp.int32, (tq, gw), 1)
    visible = kpos <= qpos
    slot = n_full % 2
    update([jnp.where(visible, s_ref[slot, h], NEG_BIG) for h in heads], n_full)

    for h in heads:
        sl = slice(h * d, (h + 1) * d)
        o = acc_ref[h, :, 0:d] / acc_ref[h, :, d:2 * d]
        o_ref[:, sl] = (o * _silu(z_ref[:, sl].astype(F32))).astype(o_ref.dtype)


def _moba_attn(proj, q_aug, k_aug, v_aug, b, t, group=4, hp=2, tq=512):
    blk = MOBA_BLOCK
    nb = t // blk
    nq = t // tq
    w = hp * HEAD_DIM
    zc = (MOBA_OFF + 3 * MOBA_W) // w
    assert nb % group == 0 and MOBA_HEADS % hp == 0 and MOBA_W % w == 0 and MOBA_OFF % w == 0
    assert tq % blk == 0 and (group * blk) % tq == 0
    resident = pl.BlockSpec((1, hp, t, 2 * HEAD_DIM), lambda bi, hi, qi: (bi, hi, 0, 0))
    return pl.pallas_call(
        functools.partial(_moba_attn_kernel, group=group, hp=hp, tq=tq),
        grid=(b, MOBA_HEADS // hp, nq),
        in_specs=[
            pl.BlockSpec((1, hp, tq, 2 * HEAD_DIM), lambda bi, hi, qi: (bi, hi, qi, 0)),
            resident,
            resident,
            pl.BlockSpec((tq, w), lambda bi, hi, qi: (bi * nq + qi, zc + hi)),
        ],
        out_specs=pl.BlockSpec((tq, w), lambda bi, hi, qi: (bi * nq + qi, hi)),
        out_shape=jax.ShapeDtypeStruct((b * t, MOBA_W), BF16),
        scratch_shapes=[pltpu.VMEM((2, hp, tq, group * blk), F32),
                        pltpu.VMEM((hp, tq, HEAD_DIM), F32),
                        pltpu.VMEM((hp, tq, 2 * HEAD_DIM), F32)],
        compiler_params=pltpu.CompilerParams(
            dimension_semantics=("parallel", "parallel", "arbitrary"), vmem_limit_bytes=VMEM_LIMIT),
        name="moba_attn",
    )(q_aug, k_aug, v_aug, proj)


def _layer(x2d, h2d, b, t, layer, w_a, w_b, conv_w, a_log, dt_bias, ret_norm_w, gdn_norm_w,
           q_norm_w, k_norm_w, w_out, next_norm_w, key_feat, q_feat):
    nb = t // MOBA_BLOCK
    proj, small = _in_proj(h2d, w_a, w_b, layer)
    y_ret = _retention(proj, ret_norm_w, b, t)
    y_gdn = _gdn(proj, small, conv_w, a_log, dt_bias, gdn_norm_w, b, t)
    k_aug, v_aug, k_mean = _moba_prep(proj, k_norm_w, key_feat, b, t)
    k_mean = jnp.pad(k_mean.reshape(b, nb, MOBA_W), ((0, 0), (0, -nb % 8), (0, 0)))
    q_aug = _moba_qprep(proj, k_mean, q_norm_w, q_feat, b, t)
    y_moba = _moba_attn(proj, q_aug, k_aug, v_aug, b, t)
    return _out_proj(x2d, y_ret, y_gdn, y_moba, w_out, layer, next_norm_w)


def kernel(x, norm_w, w_in, conv_w, a_log, dt_bias, ret_norm_w, gdn_norm_w, q_norm_w, k_norm_w, w_out):
    b, t, d = x.shape
    depth = w_in.shape[0]
    assert t % MOBA_BLOCK == 0 and t // MOBA_BLOCK <= SEL_LANES
    moba_lo = 4 * RET_W + 4 * GDN_W + 2 * GDN_HEADS
    w_a = jnp.swapaxes(w_in, 1, 2).astype(BF16)
    w_b = w_a[:, moba_lo:, :]
    w_out_b = w_out.astype(BF16)
    key_feat = _moba_key_features(t)
    q_feat = _moba_query_features()
    x2d = x.reshape(b * t, d)
    h2d = _norm(x2d, norm_w[0])
    for layer in range(depth):
        next_norm_w = norm_w[layer + 1] if layer + 1 < depth else None
        x2d, h2d = _layer(x2d, h2d, b, t, layer, w_a, w_b, conv_w[layer], a_log[layer],
                          dt_bias[layer], ret_norm_w[layer], gdn_norm_w[layer], q_norm_w[layer],
                          k_norm_w[layer], w_out_b, next_norm_w, key_feat, q_feat)
    return x2d.reshape(b, t, d)
```

```python
import functools

import numpy as np
import jax
import jax.numpy as jnp
from jax import lax
from jax.experimental import pallas as pl
from jax.experimental.pallas import tpu as pltpu

F32 = jnp.float32
BF16 = jnp.bfloat16

HEAD_DIM = 128
RET_HEADS = 4
GDN_HEADS = 6
MOBA_HEADS = 6
RET_W = RET_HEADS * HEAD_DIM
GDN_W = GDN_HEADS * HEAD_DIM
MOBA_W = MOBA_HEADS * HEAD_DIM
GDN_CONV = 4
MOBA_BLOCK = 256
MOBA_TOP_K = 3
RMS_EPS = 1e-6
SCALE = HEAD_DIM ** -0.5

GDN_OFF = 0
MOBA_OFF = 4 * GDN_W
RET_OFF = 4 * GDN_W + 4 * MOBA_W
MAIN_W = RET_OFF + 4 * RET_W
SMALL_W = 128

RET_C = 256
GDN_C = 128
HALO = 8

NEG_BIG = -1e30
SEL_LANES = 32
VMEM_LIMIT = 56 * 1024 * 1024


def _dot(a, b):
    return jnp.dot(a, b, preferred_element_type=F32)


def _dot_nt(a, b):
    return lax.dot_general(a, b, (((1,), (1,)), ((), ())), preferred_element_type=F32)


def _dot_tn(a, b):
    return lax.dot_general(a, b, (((0,), (0,)), ((), ())), preferred_element_type=F32)


def _sigmoid(x):
    return 1.0 / (1.0 + jnp.exp(-x))


def _silu(x):
    return x * _sigmoid(x)


def _split3(x):
    x1 = x.astype(BF16)
    r = x - x1.astype(F32)
    x2 = r.astype(BF16)
    x3 = (r - x2.astype(F32)).astype(BF16)
    return x1, x2, x3


def _rmsnorm_rows(x, nw):
    ms = jnp.mean(x * x, axis=-1, keepdims=True)
    return (x * lax.rsqrt(ms + RMS_EPS) * nw).astype(BF16)


def _norm_kernel(x_ref, nw_ref, h_ref):
    h_ref[...] = _rmsnorm_rows(x_ref[...], nw_ref[...])


def _norm(x2d, norm_w, tm=512):
    m, d = x2d.shape
    return pl.pallas_call(
        _norm_kernel,
        grid=(m // tm,),
        in_specs=[pl.BlockSpec((tm, d), lambda i: (i, 0)), pl.BlockSpec((1, d), lambda i: (0, 0))],
        out_specs=pl.BlockSpec((tm, d), lambda i: (i, 0)),
        out_shape=jax.ShapeDtypeStruct((m, d), BF16),
        compiler_params=pltpu.CompilerParams(
            dimension_semantics=("parallel",), vmem_limit_bytes=VMEM_LIMIT),
        name="rmsnorm",
    )(x2d, norm_w.reshape(1, d))


def _in_proj_kernel(h_ref, wa_ref, wb_ref, ws_ref, o_ref, os_ref, *, na):
    j = pl.program_id(1)

    @pl.when(j == 0)
    def _():
        os_ref[...] = _dot_nt(h_ref[...], ws_ref[0])

    @pl.when(j < na)
    def _():
        o_ref[...] = _dot_nt(h_ref[...], wa_ref[0]).astype(o_ref.dtype)

    @pl.when(j >= na)
    def _():
        o_ref[...] = _dot_nt(h_ref[...], wb_ref[0]).astype(o_ref.dtype)


def _in_proj(h2d, w_a, w_b, layer, tm=2048, tn=512):
    m, d = h2d.shape
    nr = 4 * RET_W // tn
    na = (4 * RET_W + 4 * GDN_W) // tn
    nb = 4 * MOBA_W // tn
    small_blk = na * tn // SMALL_W
    assert 4 * RET_W % tn == 0 and 4 * GDN_W % tn == 0 and 4 * MOBA_W % tn == 0 and m % tm == 0
    assert (na * tn) % SMALL_W == 0 and na * tn + SMALL_W <= w_a.shape[1]
    return pl.pallas_call(
        functools.partial(_in_proj_kernel, na=na),
        grid=(m // tm, na + nb),
        in_specs=[
            pl.BlockSpec((tm, d), lambda i, j: (i, 0)),
            pl.BlockSpec((1, tn, d), lambda i, j: (layer, jnp.minimum(j, na - 1), 0)),
            pl.BlockSpec((1, tn, d), lambda i, j: (layer, jnp.maximum(j - na, 0), 0)),
            pl.BlockSpec((1, SMALL_W, d), lambda i, j: (layer, small_blk, 0)),
        ],
        out_specs=[
            pl.BlockSpec((tm, tn), lambda i, j: (i, jnp.where(j < nr, j + (na - nr) + nb, j - nr))),
            pl.BlockSpec((tm, SMALL_W), lambda i, j: (i, 0)),
        ],
        out_shape=[
            jax.ShapeDtypeStruct((m, MAIN_W), BF16),
            jax.ShapeDtypeStruct((m, SMALL_W), F32),
        ],
        compiler_params=pltpu.CompilerParams(
            dimension_semantics=("parallel", "arbitrary"), vmem_limit_bytes=VMEM_LIMIT),
        name="in_proj",
    )(h2d, w_a, w_b, w_a)


def _out_proj_kernel(x_ref, yr_ref, yg_ref, ym_ref, w_ref, *rest, tn, with_norm):
    if with_norm:
        nw_ref, o_ref, h_ref = rest
    else:
        (o_ref,) = rest
    n = o_ref.shape[1]
    yr = yr_ref[...]
    yg = yg_ref[...]
    ym = ym_ref[...]
    for c in range(0, n, tn):
        acc = x_ref[:, c:c + tn]
        acc = acc + _dot(yr, w_ref[0, 0:RET_W, c:c + tn])
        acc = acc + _dot(yg, w_ref[0, RET_W:RET_W + GDN_W, c:c + tn])
        acc = acc + _dot(ym, w_ref[0, RET_W + GDN_W:RET_W + GDN_W + MOBA_W, c:c + tn])
        o_ref[:, c:c + tn] = acc
    if with_norm:
        h_ref[...] = _rmsnorm_rows(o_ref[...], nw_ref[...])


def _out_proj(x2d, y_ret, y_gdn, y_moba, w_out, layer, next_norm_w=None, tm=512, tn=512):
    m, d = x2d.shape
    with_norm = next_norm_w is not None
    in_specs = [
        pl.BlockSpec((tm, d), lambda i: (i, 0)),
        pl.BlockSpec((tm, RET_W), lambda i: (i, 0)),
        pl.BlockSpec((tm, GDN_W), lambda i: (i, 0)),
        pl.BlockSpec((tm, MOBA_W), lambda i: (i, 0)),
        pl.BlockSpec((1,) + w_out.shape[1:], lambda i: (layer, 0, 0)),
    ]
    args = [x2d, y_ret, y_gdn, y_moba, w_out]
    out_specs = [pl.BlockSpec((tm, d), lambda i: (i, 0))]
    out_shape = [jax.ShapeDtypeStruct((m, d), F32)]
    if with_norm:
        in_specs.append(pl.BlockSpec((1, d), lambda i: (0, 0)))
        args.append(next_norm_w.reshape(1, d))
        out_specs.append(pl.BlockSpec((tm, d), lambda i: (i, 0)))
        out_shape.append(jax.ShapeDtypeStruct((m, d), BF16))
    res = pl.pallas_call(
        functools.partial(_out_proj_kernel, tn=tn, with_norm=with_norm),
        grid=(m // tm,),
        in_specs=in_specs,
        out_specs=out_specs,
        out_shape=out_shape,
        compiler_params=pltpu.CompilerParams(
            dimension_semantics=("parallel",), vmem_limit_bytes=VMEM_LIMIT),
        name="out_proj",
    )(*args)
    return (res[0], res[1]) if with_norm else (res[0], None)


def _retention_consts(c):
    h = np.arange(RET_HEADS, dtype=np.float64)
    log_gamma = np.log1p(-np.exp2(-5.0 - h))
    pos = np.arange(c, dtype=np.float64)
    diff = pos[:, None] - pos[None, :]
    dmat = np.where(diff >= 0, np.exp(np.maximum(diff, 0.0) * log_gamma[:, None, None]), 0.0) * SCALE
    kws = np.exp((c - 1 - pos)[None, :] * log_gamma[:, None])
    qws = np.exp((pos + 1.0)[None, :] * log_gamma[:, None]) * SCALE
    kws = np.broadcast_to(kws[:, :, None], (RET_HEADS, c, HEAD_DIM))
    qws = np.broadcast_to(qws[:, :, None], (RET_HEADS, c, HEAD_DIM))
    chunk_decay = [float(v) for v in np.exp(c * log_gamma)]
    return (jnp.asarray(dmat, F32), jnp.asarray(kws, F32), jnp.asarray(qws, F32), chunk_decay)


def _retention_kernel(q_ref, k_ref, v_ref, g_ref, dm_ref, kws_ref, qws_ref, nw_ref, o_ref, st_ref,
                      *, chunk_decay, nsub):
    c = RET_C

    @pl.when(pl.program_id(1) == 0)
    def _():
        st_ref[...] = jnp.zeros_like(st_ref)

    heads = range(RET_HEADS)
    hsl = [slice(h * HEAD_DIM, (h + 1) * HEAD_DIM) for h in heads]
    units = [(s, h) for s in range(nsub) for h in heads]
    un = range(len(units))
    rows = [slice(s * c, (s + 1) * c) for s, _ in units]
    q = [q_ref[rows[u], hsl[h]] for u, (s, h) in enumerate(units)]
    k = [k_ref[rows[u], hsl[h]] for u, (s, h) in enumerate(units)]
    v = [v_ref[rows[u], hsl[h]] for u, (s, h) in enumerate(units)]
    s_b = [(_dot_nt(q[u], k[u]) * dm_ref[h]).astype(BF16)
           for u, (s, h) in enumerate(units)]
    qw = [(q[u].astype(F32) * qws_ref[h]).astype(BF16) for u, (s, h) in enumerate(units)]
    kw = [(k[u].astype(F32) * kws_ref[h]).astype(BF16) for u, (s, h) in enumerate(units)]
    intra = [_dot(s_b[u], v[u]) for u in un]
    kv = [_dot_tn(kw[u], v[u]) for u in un]

    state = [st_ref[h] for h in heads]
    o = []
    for s in range(nsub):
        us = [s * RET_HEADS + h for h in heads]
        o += [intra[us[h]] + _dot(qw[us[h]], state[h].astype(BF16)) for h in heads]
        state = [state[h] * chunk_decay[h] + kv[us[h]] for h in heads]
    for h in heads:
        st_ref[h] = state[h]

    for u, (s, h) in enumerate(units):
        ms = jnp.mean(o[u] * o[u], axis=-1, keepdims=True)
        y = o[u] * lax.rsqrt(ms + RMS_EPS) * nw_ref[h:h + 1, :]
        o_ref[rows[u], hsl[h]] = (y * _silu(g_ref[rows[u], hsl[h]].astype(F32))).astype(o_ref.dtype)


def _retention(proj, ret_norm_w, b, t, nsub=4):
    c = RET_C
    tt = nsub * c
    nt = t // tt
    dmat, kws, qws, chunk_decay = _retention_consts(c)
    base = RET_OFF // RET_W

    def col(ci):
        return pl.BlockSpec((tt, RET_W), lambda bi, ti: (bi * nt + ti, base + ci))

    def const3(shape):
        return pl.BlockSpec(shape, lambda bi, ti: (0, 0, 0))

    return pl.pallas_call(
        functools.partial(_retention_kernel, chunk_decay=chunk_decay, nsub=nsub),
        grid=(b, nt),
        in_specs=[col(0), col(1), col(2), col(3),
                  const3(dmat.shape), const3(kws.shape), const3(qws.shape),
                  pl.BlockSpec((RET_HEADS, HEAD_DIM), lambda bi, ti: (0, 0))],
        out_specs=pl.BlockSpec((tt, RET_W), lambda bi, ti: (bi * nt + ti, 0)),
        out_shape=jax.ShapeDtypeStruct((b * t, RET_W), BF16),
        scratch_shapes=[pltpu.VMEM((RET_HEADS, HEAD_DIM, HEAD_DIM), F32)],
        compiler_params=pltpu.CompilerParams(
            dimension_semantics=("parallel", "arbitrary"), vmem_limit_bytes=VMEM_LIMIT),
        name="retention",
    )(proj, proj, proj, proj, dmat, kws, qws, ret_norm_w)


def _gdn_kernel(qkv_ref, z_ref, sm_ref, cw_ref, al_ref, dtb_ref, nw_ref, tril_ref, o_ref,
                buf_ref, st_ref, *, nsub):
    c = GDN_C
    d = HEAD_DIM
    tt = nsub * c
    heads = range(GDN_HEADS)

    @pl.when(pl.program_id(1) == 0)
    def _():
        buf_ref[0:HALO, :] = jnp.zeros((HALO, buf_ref.shape[1]), F32)
        st_ref[...] = jnp.zeros_like(st_ref)

    buf_ref[HALO:HALO + tt, :] = qkv_ref[...].astype(F32)

    def conv_act(s, col):
        sl = slice(col * d, (col + 1) * d)
        r0 = HALO + s * c
        acc = cw_ref[GDN_CONV - 1:GDN_CONV, sl] * buf_ref[r0:r0 + c, sl]
        for j in range(GDN_CONV - 1):
            sh = GDN_CONV - 1 - j
            acc = acc + cw_ref[j:j + 1, sl] * buf_ref[r0 - sh:r0 - sh + c, sl]
        return _silu(acc)

    def l2n(t):
        return t * lax.rsqrt(jnp.sum(t * t, axis=-1, keepdims=True) + RMS_EPS)

    sm = sm_ref[...]
    x = sm + dtb_ref[...]
    softplus = jnp.maximum(x, 0.0) + jnp.log1p(jnp.exp(-jnp.abs(x)))
    g = -jnp.exp(al_ref[...]) * softplus
    beta = _sigmoid(sm)
    tril = tril_ref[...]
    gc, gct = [], []
    for s in range(nsub):
        g1, g2, g3 = _split3(g[s * c:(s + 1) * c])
        gcs = _dot(tril, g1) + _dot(tril, g2) + _dot(tril, g3)
        gc.append(gcs)
        gct.append(gcs.T)

    row = lax.broadcasted_iota(jnp.int32, (c, c), 0)
    coli = lax.broadcasted_iota(jnp.int32, (c, c), 1)
    causal = row >= coli
    strict = row > coli
    eye = jnp.where(row == coli, 1.0, 0.0).astype(F32)
    diag8 = strict & ((row >> 3) == (coli >> 3))
    merge_masks = [((row >> (lv + 1)) == (coli >> (lv + 1))) & ((row >> lv) != (coli >> lv)) & strict
                   for lv in range(3, 7)]

    units = [(s, h) for s in range(nsub) for h in heads]
    un = range(len(units))
    q = [l2n(conv_act(s, h)) * SCALE for s, h in units]
    k = [l2n(conv_act(s, GDN_HEADS + h)) for s, h in units]
    v = [conv_act(s, 2 * GDN_HEADS + h) for s, h in units]
    gcol = [jnp.broadcast_to(gc[s][:, h:h + 1], (c, c)) for s, h in units]
    dm = [jnp.exp(jnp.where(causal, gcol[u] - gct[s][h:h + 1, :], NEG_BIG))
          for u, (s, h) in enumerate(units)]
    bb = [jnp.broadcast_to(beta[s * c:(s + 1) * c, GDN_HEADS + h:GDN_HEADS + h + 1], (c, d))
          for s, h in units]
    eg = [jnp.exp(gcol[u]) for u in un]
    glast = [gcol[u][c - 1:c, :] for u in un]
    ek = [jnp.exp(glast[u] - gcol[u]) for u in un]
    kb = [k[u] * bb[u] for u in un]
    k_b = [k[u].astype(BF16) for u in un]

    sc = [_dot_nt(jnp.concatenate([kb[u], q[u]], axis=0).astype(BF16), k_b[u]) for u in un]
    mm = [jnp.where(strict, sc[u][:c] * dm[u], 0.0) for u in un]
    qk_b = [(sc[u][c:] * dm[u]).astype(BF16) for u in un]

    md = [jnp.where(diag8, mm[u], 0.0) for u in un]
    md_b = [md[u].astype(BF16) for u in un]
    x0 = [eye - md[u] for u in un]
    md2_b = [_dot(md_b[u], md_b[u]).astype(BF16) for u in un]
    t1 = [_dot(md2_b[u], jnp.concatenate([x0[u].astype(BF16), md2_b[u]], axis=-1)) for u in un]
    x1 = [x0[u] + t1[u][:, :c] for u in un]
    md4_b = [t1[u][:, c:].astype(BF16) for u in un]
    xi = [x1[u] + _dot(md4_b[u], x1[u].astype(BF16)) for u in un]
    for mask in merge_masks:
        mo_b = [jnp.where(mask, mm[u], 0.0).astype(BF16) for u in un]
        xi_b = [xi[u].astype(BF16) for u in un]
        y_b = [_dot(xi_b[u], mo_b[u]).astype(BF16) for u in un]
        xi = [xi[u] - _dot(y_b[u], xi_b[u]) for u in un]
    xc_b = [(xi[u] - eye).astype(BF16) for u in un]

    rhs = [jnp.concatenate([kb[u] * eg[u], v[u] * bb[u]], axis=-1) for u in un]
    wu = [rhs[u] + _dot(xc_b[u], rhs[u].astype(BF16)) for u in un]
    wq_b = [jnp.concatenate([wu[u][:, :d], q[u] * eg[u]], axis=0).astype(BF16) for u in un]
    kg_b = [(k[u] * ek[u]).astype(BF16) for u in un]
    elast = [jnp.exp(glast[u]) for u in un]

    state = [st_ref[h] for h in heads]
    for s in range(nsub):
        us = [s * GDN_HEADS + h for h in heads]
        state_b = [state[h].astype(BF16) for h in heads]
        ws = [_dot(wq_b[us[h]], state_b[h]) for h in heads]
        v_new_b = [(wu[us[h]][:, d:] - ws[h][:c]).astype(BF16) for h in heads]
        o = [ws[h][c:] + _dot(qk_b[us[h]], v_new_b[h]) for h in heads]
        state = [state[h] * elast[us[h]] + _dot_tn(kg_b[us[h]], v_new_b[h]) for h in heads]
        for h in heads:
            sl = slice(h * d, (h + 1) * d)
            rows = slice(s * c, (s + 1) * c)
            ms = jnp.mean(o[h] * o[h], axis=-1, keepdims=True)
            y = o[h] * lax.rsqrt(ms + RMS_EPS) * nw_ref[...]
            o_ref[rows, sl] = (y * _silu(z_ref[rows, sl].astype(F32))).astype(o_ref.dtype)
    for h in heads:
        st_ref[h] = state[h]

    buf_ref[0:HALO, :] = buf_ref[tt:tt + HALO, :]


def _gdn(proj, small, conv_w, a_log, dt_bias, gdn_norm_w, b, t, nsub=2):
    c = GDN_C
    tt = nsub * c
    nt = t // tt
    qkv_w = 3 * GDN_W
    pad = SMALL_W - GDN_HEADS
    al_row = jnp.pad(a_log.astype(F32), (0, pad)).reshape(1, SMALL_W)
    dtb_row = jnp.pad(dt_bias.astype(F32), (0, pad)).reshape(1, SMALL_W)
    tril = jnp.asarray(np.tril(np.ones((c, c), np.float32)), BF16)
    return pl.pallas_call(
        functools.partial(_gdn_kernel, nsub=nsub),
        grid=(b, nt),
        in_specs=[
            pl.BlockSpec((tt, qkv_w), lambda bi, ti: (bi * nt + ti, GDN_OFF // qkv_w)),
            pl.BlockSpec((tt, GDN_W), lambda bi, ti: (bi * nt + ti, (GDN_OFF + qkv_w) // GDN_W)),
            pl.BlockSpec((tt, SMALL_W), lambda bi, ti: (bi * nt + ti, 0)),
            pl.BlockSpec((GDN_CONV, qkv_w), lambda bi, ti: (0, 0)),
            pl.BlockSpec((1, SMALL_W), lambda bi, ti: (0, 0)),
            pl.BlockSpec((1, SMALL_W), lambda bi, ti: (0, 0)),
            pl.BlockSpec((1, HEAD_DIM), lambda bi, ti: (0, 0)),
            pl.BlockSpec((c, c), lambda bi, ti: (0, 0)),
        ],
        out_specs=pl.BlockSpec((tt, GDN_W), lambda bi, ti: (bi * nt + ti, 0)),
        out_shape=jax.ShapeDtypeStruct((b * t, GDN_W), BF16),
        scratch_shapes=[pltpu.VMEM((HALO + tt, qkv_w), F32),
                        pltpu.VMEM((GDN_HEADS, HEAD_DIM, HEAD_DIM), F32)],
        compiler_params=pltpu.CompilerParams(
            dimension_semantics=("parallel", "arbitrary"), vmem_limit_bytes=VMEM_LIMIT),
        name="gdn",
    )(proj, proj, small, conv_w, al_row, dtb_row, gdn_norm_w.reshape(1, HEAD_DIM), tril)


def _moba_key_features(t):
    pos = np.arange(t)
    blk = pos // MOBA_BLOCK
    off = pos % MOBA_BLOCK
    f = np.zeros((t, HEAD_DIM), np.float32)
    f[pos, blk] = 1.0
    f[:, SEL_LANES] = 1.0
    for p in range(3):
        f[:, SEL_LANES + 1 + p] = blk
        f[:, SEL_LANES + 4 + p] = off // 16
        f[:, SEL_LANES + 7 + p] = off % 16
    return jnp.asarray(f, BF16)


def _moba_query_features():
    hh = MOBA_HEADS
    slopes = np.exp2(-8.0 * np.arange(1, hh + 1, dtype=np.float64) / hh).astype(np.float32)
    s = jnp.asarray(slopes, F32)
    s1 = s.astype(BF16).astype(F32)
    s2 = (s - s1).astype(BF16).astype(F32)
    s3 = (s - s1 - s2).astype(BF16).astype(F32)
    f = jnp.zeros((hh, HEAD_DIM), F32)
    f = f.at[:, SEL_LANES].set(-s * MOBA_BLOCK)
    for p, sp in enumerate((s1, s2, s3)):
        f = f.at[:, SEL_LANES + 1 + p].set(sp * MOBA_BLOCK)
        f = f.at[:, SEL_LANES + 4 + p].set(sp * 16.0)
        f = f.at[:, SEL_LANES + 7 + p].set(sp)
    return f


def _moba_prep_kernel(k_ref, v_ref, nw_ref, pf_ref, ka_ref, va_ref, km_ref):
    ones = jnp.ones((k_ref.shape[0], HEAD_DIM), BF16)
    for h in range(MOBA_HEADS):
        sl = slice(h * HEAD_DIM, (h + 1) * HEAD_DIM)
        k = k_ref[:, sl].astype(F32)
        ms = jnp.mean(k * k, axis=-1, keepdims=True)
        kn = k * lax.rsqrt(ms + RMS_EPS) * nw_ref[...]
        ka_ref[0, h, :, 0:HEAD_DIM] = kn.astype(BF16)
        ka_ref[0, h, :, HEAD_DIM:2 * HEAD_DIM] = pf_ref[...]
        for j in range(k_ref.shape[0] // MOBA_BLOCK):
            rows = slice(j * MOBA_BLOCK, (j + 1) * MOBA_BLOCK)
            km_ref[0, j, :, sl] = jnp.mean(kn[rows], axis=0, keepdims=True)
        va_ref[0, h, :, 0:HEAD_DIM] = v_ref[:, sl]
        va_ref[0, h, :, HEAD_DIM:2 * HEAD_DIM] = ones


def _moba_prep(proj, k_norm_w, key_feat, b, t, bpt=4):
    blk = bpt * MOBA_BLOCK
    nb = t // blk
    kc = (MOBA_OFF + MOBA_W) // MOBA_W
    aug = pl.BlockSpec((1, MOBA_HEADS, blk, 2 * HEAD_DIM), lambda bi, ti: (bi, 0, ti, 0))
    aug_shape = jax.ShapeDtypeStruct((b, MOBA_HEADS, t, 2 * HEAD_DIM), BF16)
    return pl.pallas_call(
        _moba_prep_kernel,
        grid=(b, nb),
        in_specs=[
            pl.BlockSpec((blk, MOBA_W), lambda bi, ti: (bi * nb + ti, kc)),
            pl.BlockSpec((blk, MOBA_W), lambda bi, ti: (bi * nb + ti, kc + 1)),
            pl.BlockSpec((1, HEAD_DIM), lambda bi, ti: (0, 0)),
            pl.BlockSpec((blk, HEAD_DIM), lambda bi, ti: (ti, 0)),
        ],
        out_specs=[aug, aug, pl.BlockSpec((1, bpt, 1, MOBA_W), lambda bi, ti: (bi, ti, 0, 0))],
        out_shape=[aug_shape, aug_shape, jax.ShapeDtypeStruct((b, nb * bpt, 1, MOBA_W), F32)],
        compiler_params=pltpu.CompilerParams(
            dimension_semantics=("parallel", "parallel"), vmem_limit_bytes=VMEM_LIMIT),
        name="moba_prep",
    )(proj, proj, k_norm_w.reshape(1, HEAD_DIM), key_feat)


def _moba_qprep_kernel(q_ref, km_ref, qnw_ref, qf_ref, qa_ref):
    tq = q_ref.shape[0]
    nbp = km_ref.shape[1]
    t0 = pl.program_id(1) * tq
    lane = lax.broadcasted_iota(jnp.int32, (tq, HEAD_DIM), 1)
    rowi = lax.broadcasted_iota(jnp.int32, (tq, HEAD_DIM), 0)
    own = (t0 + rowi) // MOBA_BLOCK
    own_coef = jnp.where(lane == SEL_LANES, own.astype(F32), 1.0)
    blk_t = lax.broadcasted_iota(jnp.int32, (nbp, tq), 0)
    blk_f = blk_t.astype(F32)
    own_t = (t0 + lax.broadcasted_iota(jnp.int32, (nbp, tq), 1)) // MOBA_BLOCK
    past = blk_t < own_t
    heads = range(MOBA_HEADS)
    hsl = [slice(h * HEAD_DIM, (h + 1) * HEAD_DIM) for h in heads]

    def qnorm(h):
        q = q_ref[:, hsl[h]].astype(F32)
        ms = jnp.mean(q * q, axis=-1, keepdims=True)
        return q * lax.rsqrt(ms + RMS_EPS) * qnw_ref[...]

    qn = [qnorm(h) for h in heads]
    for h in heads:
        qa_ref[0, h, :, 0:HEAD_DIM] = (qn[h] * SCALE).astype(BF16)

    def gate_scores(h):
        km = km_ref[0, :, hsl[h]]
        q1 = qn[h].astype(BF16)
        q2 = (qn[h] - q1.astype(F32)).astype(BF16)
        k1 = km.astype(BF16)
        k2 = (km - k1.astype(F32)).astype(BF16)
        return jnp.where(past, _dot_nt(k1, q1) + _dot_nt(k2, q1) + _dot_nt(k1, q2), -jnp.inf)

    gate = [gate_scores(h) for h in heads]
    taken = [jnp.zeros((nbp, tq), F32) for _ in heads]
    no_block = float(nbp)
    for _ in range(MOBA_TOP_K):
        gm = [jnp.where(taken[h] > 0.0, -jnp.inf, gate[h]) for h in heads]
        mx = [jnp.max(gm[h], axis=0, keepdims=True) for h in heads]
        cand = [jnp.where(taken[h] > 0.0, no_block,
                          jnp.where(gm[h] == mx[h], blk_f, no_block)) for h in heads]
        idx = [jnp.min(cand[h], axis=0, keepdims=True) for h in heads]
        taken = [jnp.where(blk_f == idx[h], 1.0, taken[h]) for h in heads]
    zero_rows = jnp.zeros((HEAD_DIM - nbp, tq), F32)
    for h in heads:
        sel_t = jnp.where(blk_t == own_t, 0.0,
                          jnp.where(past, jnp.where(taken[h] > 0.0, 0.0, NEG_BIG), NEG_BIG))
        sel_bias = jnp.concatenate([sel_t, zero_rows], axis=0).T
        extra = jnp.where(lane < SEL_LANES, sel_bias, qf_ref[h:h + 1, :] * own_coef)
        qa_ref[0, h, :, HEAD_DIM:2 * HEAD_DIM] = extra.astype(BF16)


def _moba_qprep(proj, k_mean, q_norm_w, q_feat, b, t, tq=512):
    nq = t // tq
    nbp = k_mean.shape[1]
    return pl.pallas_call(
        _moba_qprep_kernel,
        grid=(b, nq),
        in_specs=[
            pl.BlockSpec((tq, MOBA_W), lambda bi, ti: (bi * nq + ti, MOBA_OFF // MOBA_W)),
            pl.BlockSpec((1, nbp, MOBA_W), lambda bi, ti: (bi, 0, 0)),
            pl.BlockSpec((1, HEAD_DIM), lambda bi, ti: (0, 0)),
            pl.BlockSpec((MOBA_HEADS, HEAD_DIM), lambda bi, ti: (0, 0)),
        ],
        out_specs=pl.BlockSpec((1, MOBA_HEADS, tq, 2 * HEAD_DIM), lambda bi, ti: (bi, 0, ti, 0)),
        out_shape=jax.ShapeDtypeStruct((b, MOBA_HEADS, t, 2 * HEAD_DIM), BF16),
        compiler_params=pltpu.CompilerParams(
            dimension_semantics=("parallel", "parallel"), vmem_limit_bytes=VMEM_LIMIT),
        name="moba_qprep",
    )(proj, k_mean, q_norm_w.reshape(1, HEAD_DIM), q_feat)


def _moba_attn_kernel(qa_ref, ka_ref, va_ref, z_ref, o_ref, s_ref, m_ref, acc_ref, *, group, hp, tq):
    blk = MOBA_BLOCK
    d = HEAD_DIM
    gw = group * blk
    qi = pl.program_id(2)
    n_full = (qi * tq) // gw
    heads = range(hp)

    m_ref[...] = jnp.full(m_ref.shape, NEG_BIG, F32)
    acc_ref[...] = jnp.zeros(acc_ref.shape, F32)

    def store_scores(slot, g):
        off = pl.multiple_of(g * gw, gw)
        for h in heads:
            s_ref[slot, h] = _dot_nt(qa_ref[0, h], ka_ref[0, h, pl.ds(off, gw), :])

    def update(s, g):
        off = pl.multiple_of(g * gw, gw)
        m_prev = [m_ref[h] for h in heads]
        m_new = [jnp.maximum(m_prev[h], jnp.max(s[h], axis=-1, keepdims=True)) for h in heads]
        alpha = [jnp.exp(m_prev[h] - m_new[h]) for h in heads]
        p = [jnp.exp(s[h] - jnp.concatenate([m_new[h]] * (gw // d), axis=-1)).astype(BF16)
             for h in heads]
        for h in heads:
            m_ref[h] = m_new[h]
        pv = [_dot(p[h], va_ref[0, h, pl.ds(off, gw), :]) for h in heads]
        for h in heads:
            acc_ref[h] = jnp.concatenate([alpha[h], alpha[h]], axis=-1) * acc_ref[h] + pv[h]

    def step(slot, g):
        s = [s_ref[slot, h] for h in heads]
        update(s, g)
        store_scores(1 - slot, g + 1)

    store_scores(0, 0)

    def body(i, carry):
        step(0, 2 * i)
        step(1, 2 * i + 1)
        return carry

    lax.fori_loop(0, n_full // 2, body, 0)

    @pl.when(n_full % 2 == 1)
    def _():
        step(0, n_full - 1)

    qpos = qi * tq + lax.broadcasted_iota(jnp.int32, (tq, gw), 0)
    kpos = n_full * gw + lax.broadcasted_iota(jnp.int32, (tq, gw), 1)
    visible = kpos <= qpos
    slot = n_full % 2
    update([jnp.where(visible, s_ref[slot, h], NEG_BIG) for h in heads], n_full)

    for h in heads:
        sl = slice(h * d, (h + 1) * d)
        o = acc_ref[h, :, 0:d] / acc_ref[h, :, d:2 * d]
        o_ref[:, sl] = (o * _silu(z_ref[:, sl].astype(F32))).astype(o_ref.dtype)


def _moba_attn(proj, q_aug, k_aug, v_aug, b, t, group=4, hp=2, tq=512):
    blk = MOBA_BLOCK
    nb = t // blk
    nq = t // tq
    w = hp * HEAD_DIM
    zc = (MOBA_OFF + 3 * MOBA_W) // w
    assert nb % group == 0 and MOBA_HEADS % hp == 0 and MOBA_W % w == 0 and MOBA_OFF % w == 0
    assert tq % blk == 0 and (group * blk) % tq == 0
    resident = pl.BlockSpec((1, hp, t, 2 * HEAD_DIM), lambda bi, hi, qi: (bi, hi, 0, 0))
    return pl.pallas_call(
        functools.partial(_moba_attn_kernel, group=group, hp=hp, tq=tq),
        grid=(b, MOBA_HEADS // hp, nq),
        in_specs=[
            pl.BlockSpec((1, hp, tq, 2 * HEAD_DIM), lambda bi, hi, qi: (bi, hi, qi, 0)),
            resident,
            resident,
            pl.BlockSpec((tq, w), lambda bi, hi, qi: (bi * nq + qi, zc + hi)),
        ],
        out_specs=pl.BlockSpec((tq, w), lambda bi, hi, qi: (bi * nq + qi, hi)),
        out_shape=jax.ShapeDtypeStruct((b * t, MOBA_W), BF16),
        scratch_shapes=[pltpu.VMEM((2, hp, tq, group * blk), F32),
                        pltpu.VMEM((hp, tq, HEAD_DIM), F32),
                        pltpu.VMEM((hp, tq, 2 * HEAD_DIM), F32)],
        compiler_params=pltpu.CompilerParams(
            dimension_semantics=("parallel", "parallel", "arbitrary"), vmem_limit_bytes=VMEM_LIMIT),
        name="moba_attn",
    )(q_aug, k_aug, v_aug, proj)


def _layer(x2d, h2d, b, t, layer, w_a, w_b, conv_w, a_log, dt_bias, ret_norm_w, gdn_norm_w,
           q_norm_w, k_norm_w, w_out, next_norm_w, key_feat, q_feat):
    nb = t // MOBA_BLOCK
    proj, small = _in_proj(h2d, w_a, w_b, layer)
    y_ret = _retention(proj, ret_norm_w, b, t)
    y_gdn = _gdn(proj, small, conv_w, a_log, dt_bias, gdn_norm_w, b, t)
    k_aug, v_aug, k_mean = _moba_prep(proj, k_norm_w, key_feat, b, t)
    k_mean = jnp.pad(k_mean.reshape(b, nb, MOBA_W), ((0, 0), (0, -nb % 8), (0, 0)))
    q_aug = _moba_qprep(proj, k_mean, q_norm_w, q_feat, b, t)
    y_moba = _moba_attn(proj, q_aug, k_aug, v_aug, b, t)
    return _out_proj(x2d, y_ret, y_gdn, y_moba, w_out, layer, next_norm_w)


def kernel(x, norm_w, w_in, conv_w, a_log, dt_bias, ret_norm_w, gdn_norm_w, q_norm_w, k_norm_w, w_out):
    b, t, d = x.shape
    depth = w_in.shape[0]
    assert t % MOBA_BLOCK == 0 and t // MOBA_BLOCK <= SEL_LANES
    moba_lo = 4 * RET_W + 4 * GDN_W + 2 * GDN_HEADS
    w_a = jnp.swapaxes(w_in, 1, 2).astype(BF16)
    w_b = w_a[:, moba_lo:, :]
    w_out_b = w_out.astype(BF16)
    key_feat = _moba_key_features(t)
    q_feat = _moba_query_features()
    x2d = x.reshape(b * t, d)
    h2d = _norm(x2d, norm_w[0])
    for layer in range(depth):
        next_norm_w = norm_w[layer + 1] if layer + 1 < depth else None
        x2d, h2d = _layer(x2d, h2d, b, t, layer, w_a, w_b, conv_w[layer], a_log[layer],
                          dt_bias[layer], ret_norm_w[layer], gdn_norm_w[layer], q_norm_w[layer],
                          k_norm_w[layer], w_out_b, next_norm_w, key_feat, q_feat)
    return x2d.reshape(b, t, d)
```

```python
import functools

import numpy as np
import jax
import jax.numpy as jnp
from jax import lax
from jax.experimental import pallas as pl
from jax.experimental.pallas import tpu as pltpu

F32 = jnp.float32
BF16 = jnp.bfloat16

HEAD_DIM = 128
RET_HEADS = 4
GDN_HEADS = 6
MOBA_HEADS = 6
RET_W = RET_HEADS * HEAD_DIM
GDN_W = GDN_HEADS * HEAD_DIM
MOBA_W = MOBA_HEADS * HEAD_DIM
GDN_CONV = 4
MOBA_BLOCK = 256
MOBA_TOP_K = 3
RMS_EPS = 1e-6
SCALE = HEAD_DIM ** -0.5
LOG2_E = 1.4426950408889634

GDN_OFF = 0
MOBA_OFF = 4 * GDN_W
RET_OFF = 4 * GDN_W + 4 * MOBA_W
MAIN_W = RET_OFF + 4 * RET_W
SMALL_W = 128

RET_C = 256
GDN_C = 128
HALO = 8

NEG_BIG = -1e30
SEL_LANES = 32
VMEM_LIMIT = 56 * 1024 * 1024


def _dot(a, b):
    return jnp.dot(a, b, preferred_element_type=F32)


def _dot_nt(a, b):
    return lax.dot_general(a, b, (((1,), (1,)), ((), ())), preferred_element_type=F32)


def _dot_tn(a, b):
    return lax.dot_general(a, b, (((0,), (0,)), ((), ())), preferred_element_type=F32)


def _sigmoid(x):
    return 1.0 / (1.0 + jnp.exp(-x))


def _silu(x):
    return x * _sigmoid(x)


def _split3(x):
    x1 = x.astype(BF16)
    r = x - x1.astype(F32)
    x2 = r.astype(BF16)
    x3 = (r - x2.astype(F32)).astype(BF16)
    return x1, x2, x3


def _rmsnorm_rows(x, nw):
    ms = jnp.mean(x * x, axis=-1, keepdims=True)
    return (x * lax.rsqrt(ms + RMS_EPS) * nw).astype(BF16)


def _norm_kernel(x_ref, nw_ref, h_ref):
    h_ref[...] = _rmsnorm_rows(x_ref[...], nw_ref[...])


def _norm(x2d, norm_w, tm=512):
    m, d = x2d.shape
    return pl.pallas_call(
        _norm_kernel,
        grid=(m // tm,),
        in_specs=[pl.BlockSpec((tm, d), lambda i: (i, 0)), pl.BlockSpec((1, d), lambda i: (0, 0))],
        out_specs=pl.BlockSpec((tm, d), lambda i: (i, 0)),
        out_shape=jax.ShapeDtypeStruct((m, d), BF16),
        compiler_params=pltpu.CompilerParams(
            dimension_semantics=("parallel",), vmem_limit_bytes=VMEM_LIMIT),
        name="rmsnorm",
    )(x2d, norm_w.reshape(1, d))


def _in_proj_kernel(h_ref, wa_ref, wb_ref, ws_ref, o_ref, os_ref, *, na):
    j = pl.program_id(1)

    @pl.when(j == 0)
    def _():
        os_ref[...] = _dot_nt(h_ref[...], ws_ref[0])

    @pl.when(j < na)
    def _():
        o_ref[...] = _dot_nt(h_ref[...], wa_ref[0]).astype(o_ref.dtype)

    @pl.when(j >= na)
    def _():
        o_ref[...] = _dot_nt(h_ref[...], wb_ref[0]).astype(o_ref.dtype)


def _in_proj(h2d, w_a, w_b, layer, tm=2048, tn=512):
    m, d = h2d.shape
    nr = 4 * RET_W // tn
    na = (4 * RET_W + 4 * GDN_W) // tn
    nb = 4 * MOBA_W // tn
    small_blk = na * tn // SMALL_W
    assert 4 * RET_W % tn == 0 and 4 * GDN_W % tn == 0 and 4 * MOBA_W % tn == 0 and m % tm == 0
    assert (na * tn) % SMALL_W == 0 and na * tn + SMALL_W <= w_a.shape[1]
    return pl.pallas_call(
        functools.partial(_in_proj_kernel, na=na),
        grid=(m // tm, na + nb),
        in_specs=[
            pl.BlockSpec((tm, d), lambda i, j: (i, 0)),
            pl.BlockSpec((1, tn, d), lambda i, j: (layer, jnp.minimum(j, na - 1), 0)),
            pl.BlockSpec((1, tn, d), lambda i, j: (layer, jnp.maximum(j - na, 0), 0)),
            pl.BlockSpec((1, SMALL_W, d), lambda i, j: (layer, small_blk, 0)),
        ],
        out_specs=[
            pl.BlockSpec((tm, tn), lambda i, j: (i, jnp.where(j < nr, j + (na - nr) + nb, j - nr))),
            pl.BlockSpec((tm, SMALL_W), lambda i, j: (i, 0)),
        ],
        out_shape=[
            jax.ShapeDtypeStruct((m, MAIN_W), BF16),
            jax.ShapeDtypeStruct((m, SMALL_W), F32),
        ],
        compiler_params=pltpu.CompilerParams(
            dimension_semantics=("parallel", "arbitrary"), vmem_limit_bytes=VMEM_LIMIT),
        name="in_proj",
    )(h2d, w_a, w_b, w_a)


def _out_proj_kernel(x_ref, yr_ref, yg_ref, ym_ref, w_ref, *rest, tn, with_norm):
    if with_norm:
        nw_ref, o_ref, h_ref = rest
    else:
        (o_ref,) = rest
    n = o_ref.shape[1]
    yr = yr_ref[...]
    yg = yg_ref[...]
    ym = ym_ref[...]
    for c in range(0, n, tn):
        acc = x_ref[:, c:c + tn]
        acc = acc + _dot(yr, w_ref[0, 0:RET_W, c:c + tn])
        acc = acc + _dot(yg, w_ref[0, RET_W:RET_W + GDN_W, c:c + tn])
        acc = acc + _dot(ym, w_ref[0, RET_W + GDN_W:RET_W + GDN_W + MOBA_W, c:c + tn])
        o_ref[:, c:c + tn] = acc
    if with_norm:
        h_ref[...] = _rmsnorm_rows(o_ref[...], nw_ref[...])


def _out_proj(x2d, y_ret, y_gdn, y_moba, w_out, layer, next_norm_w=None, tm=512, tn=512):
    m, d = x2d.shape
    with_norm = next_norm_w is not None
    in_specs = [
        pl.BlockSpec((tm, d), lambda i: (i, 0)),
        pl.BlockSpec((tm, RET_W), lambda i: (i, 0)),
        pl.BlockSpec((tm, GDN_W), lambda i: (i, 0)),
        pl.BlockSpec((tm, MOBA_W), lambda i: (i, 0)),
        pl.BlockSpec((1,) + w_out.shape[1:], lambda i: (layer, 0, 0)),
    ]
    args = [x2d, y_ret, y_gdn, y_moba, w_out]
    out_specs = [pl.BlockSpec((tm, d), lambda i: (i, 0))]
    out_shape = [jax.ShapeDtypeStruct((m, d), F32)]
    if with_norm:
        in_specs.append(pl.BlockSpec((1, d), lambda i: (0, 0)))
        args.append(next_norm_w.reshape(1, d))
        out_specs.append(pl.BlockSpec((tm, d), lambda i: (i, 0)))
        out_shape.append(jax.ShapeDtypeStruct((m, d), BF16))
    res = pl.pallas_call(
        functools.partial(_out_proj_kernel, tn=tn, with_norm=with_norm),
        grid=(m // tm,),
        in_specs=in_specs,
        out_specs=out_specs,
        out_shape=out_shape,
        compiler_params=pltpu.CompilerParams(
            dimension_semantics=("parallel",), vmem_limit_bytes=VMEM_LIMIT),
        name="out_proj",
    )(*args)
    return (res[0], res[1]) if with_norm else (res[0], None)


def _retention_consts(c):
    h = np.arange(RET_HEADS, dtype=np.float64)
    log_gamma = np.log1p(-np.exp2(-5.0 - h))
    pos = np.arange(c, dtype=np.float64)
    diff = pos[:, None] - pos[None, :]
    dmat = np.where(diff >= 0, np.exp(np.maximum(diff, 0.0) * log_gamma[:, None, None]), 0.0) * SCALE
    kws = np.exp((c - 1 - pos)[None, :] * log_gamma[:, None])
    qws = np.exp((pos + 1.0)[None, :] * log_gamma[:, None]) * SCALE
    kws = np.broadcast_to(kws[:, :, None], (RET_HEADS, c, HEAD_DIM))
    qws = np.broadcast_to(qws[:, :, None], (RET_HEADS, c, HEAD_DIM))
    chunk_decay = [float(v) for v in np.exp(c * log_gamma)]
    return (jnp.asarray(dmat, F32), jnp.asarray(kws, F32), jnp.asarray(qws, F32), chunk_decay)


def _retention_kernel(q_ref, k_ref, v_ref, g_ref, dm_ref, kws_ref, qws_ref, nw_ref, o_ref, st_ref,
                      *, chunk_decay, nsub):
    c = RET_C

    @pl.when(pl.program_id(1) == 0)
    def _():
        st_ref[...] = jnp.zeros_like(st_ref)

    heads = range(RET_HEADS)
    hsl = [slice(h * HEAD_DIM, (h + 1) * HEAD_DIM) for h in heads]
    units = [(s, h) for s in range(nsub) for h in heads]
    un = range(len(units))
    rows = [slice(s * c, (s + 1) * c) for s, _ in units]
    q = [q_ref[rows[u], hsl[h]] for u, (s, h) in enumerate(units)]
    k = [k_ref[rows[u], hsl[h]] for u, (s, h) in enumerate(units)]
    v = [v_ref[rows[u], hsl[h]] for u, (s, h) in enumerate(units)]
    s_b = [(_dot_nt(q[u], k[u]) * dm_ref[h]).astype(BF16)
           for u, (s, h) in enumerate(units)]
    qw = [(q[u].astype(F32) * qws_ref[h]).astype(BF16) for u, (s, h) in enumerate(units)]
    kw = [(k[u].astype(F32) * kws_ref[h]).astype(BF16) for u, (s, h) in enumerate(units)]
    intra = [_dot(s_b[u], v[u]) for u in un]
    kv = [_dot_tn(kw[u], v[u]) for u in un]

    state = [st_ref[h] for h in heads]
    o = []
    for s in range(nsub):
        us = [s * RET_HEADS + h for h in heads]
        o += [intra[us[h]] + _dot(qw[us[h]], state[h].astype(BF16)) for h in heads]
        state = [state[h] * chunk_decay[h] + kv[us[h]] for h in heads]
    for h in heads:
        st_ref[h] = state[h]

    for u, (s, h) in enumerate(units):
        ms = jnp.mean(o[u] * o[u], axis=-1, keepdims=True)
        y = o[u] * lax.rsqrt(ms + RMS_EPS) * nw_ref[h:h + 1, :]
        o_ref[rows[u], hsl[h]] = (y * _silu(g_ref[rows[u], hsl[h]].astype(F32))).astype(o_ref.dtype)


def _retention(proj, ret_norm_w, b, t, nsub=4):
    c = RET_C
    tt = nsub * c
    nt = t // tt
    dmat, kws, qws, chunk_decay = _retention_consts(c)
    base = RET_OFF // RET_W

    def col(ci):
        return pl.BlockSpec((tt, RET_W), lambda bi, ti: (bi * nt + ti, base + ci))

    def const3(shape):
        return pl.BlockSpec(shape, lambda bi, ti: (0, 0, 0))

    return pl.pallas_call(
        functools.partial(_retention_kernel, chunk_decay=chunk_decay, nsub=nsub),
        grid=(b, nt),
        in_specs=[col(0), col(1), col(2), col(3),
                  const3(dmat.shape), const3(kws.shape), const3(qws.shape),
                  pl.BlockSpec((RET_HEADS, HEAD_DIM), lambda bi, ti: (0, 0))],
        out_specs=pl.BlockSpec((tt, RET_W), lambda bi, ti: (bi * nt + ti, 0)),
        out_shape=jax.ShapeDtypeStruct((b * t, RET_W), BF16),
        scratch_shapes=[pltpu.VMEM((RET_HEADS, HEAD_DIM, HEAD_DIM), F32)],
        compiler_params=pltpu.CompilerParams(
            dimension_semantics=("parallel", "arbitrary"), vmem_limit_bytes=VMEM_LIMIT),
        name="retention",
    )(proj, proj, proj, proj, dmat, kws, qws, ret_norm_w)


def _gdn_kernel(qkv_ref, z_ref, sm_ref, cw_ref, al_ref, dtb_ref, nw_ref, tril_ref, o_ref,
                buf_ref, st_ref, *, nsub):
    c = GDN_C
    d = HEAD_DIM
    tt = nsub * c
    heads = range(GDN_HEADS)

    @pl.when(pl.program_id(1) == 0)
    def _():
        buf_ref[0:HALO, :] = jnp.zeros((HALO, buf_ref.shape[1]), F32)
        st_ref[...] = jnp.zeros_like(st_ref)

    buf_ref[HALO:HALO + tt, :] = qkv_ref[...].astype(F32)

    def conv_act(s, col):
        sl = slice(col * d, (col + 1) * d)
        r0 = HALO + s * c
        acc = cw_ref[GDN_CONV - 1:GDN_CONV, sl] * buf_ref[r0:r0 + c, sl]
        for j in range(GDN_CONV - 1):
            sh = GDN_CONV - 1 - j
            acc = acc + cw_ref[j:j + 1, sl] * buf_ref[r0 - sh:r0 - sh + c, sl]
        return _silu(acc)

    def l2n(t, scale=1.0):
        return t * (lax.rsqrt(jnp.sum(t * t, axis=-1, keepdims=True) + RMS_EPS) * scale)

    sm = sm_ref[...]
    x = sm + dtb_ref[...]
    softplus = jnp.maximum(x, 0.0) + jnp.log1p(jnp.exp(-jnp.abs(x)))
    g = (-LOG2_E * jnp.exp(al_ref[...])) * softplus
    beta = _sigmoid(sm)
    tril = tril_ref[...]
    gc, gct = [], []
    for s in range(nsub):
        g1, g2, g3 = _split3(g[s * c:(s + 1) * c])
        gcs = _dot(tril, g1) + _dot(tril, g2) + _dot(tril, g3)
        gc.append(gcs)
        gct.append(gcs.T)

    row = lax.broadcasted_iota(jnp.int32, (c, c), 0)
    coli = lax.broadcasted_iota(jnp.int32, (c, c), 1)
    causal = row >= coli
    strict = row > coli
    eye = jnp.where(row == coli, 1.0, 0.0).astype(F32)
    diag8 = strict & ((row >> 3) == (coli >> 3))
    merge_masks = [((row >> (lv + 1)) == (coli >> (lv + 1))) & ((row >> lv) != (coli >> lv)) & strict
                   for lv in range(3, 7)]

    units = [(s, h) for s in range(nsub) for h in heads]
    un = range(len(units))
    q = [l2n(conv_act(s, h), SCALE) for s, h in units]
    k = [l2n(conv_act(s, GDN_HEADS + h)) for s, h in units]
    v = [conv_act(s, 2 * GDN_HEADS + h) for s, h in units]
    gcol = [jnp.broadcast_to(gc[s][:, h:h + 1], (c, c)) for s, h in units]
    dm = [jnp.exp2(jnp.where(causal, gcol[u] - gct[s][h:h + 1, :], NEG_BIG))
          for u, (s, h) in enumerate(units)]
    bb = [jnp.broadcast_to(beta[s * c:(s + 1) * c, GDN_HEADS + h:GDN_HEADS + h + 1], (c, d))
          for s, h in units]
    eg = [jnp.exp2(gcol[u]) for u in un]
    glast = [gcol[u][c - 1:c, :] for u in un]
    ek = [jnp.exp2(glast[u] - gcol[u]) for u in un]
    kb = [k[u] * bb[u] for u in un]
    k_b = [k[u].astype(BF16) for u in un]

    sc = [_dot_nt(jnp.concatenate([kb[u], q[u]], axis=0).astype(BF16), k_b[u]) for u in un]
    mm = [jnp.where(strict, sc[u][:c] * dm[u], 0.0) for u in un]
    qk_b = [(sc[u][c:] * dm[u]).astype(BF16) for u in un]

    md = [jnp.where(diag8, mm[u], 0.0) for u in un]
    md_b = [md[u].astype(BF16) for u in un]
    x0 = [eye - md[u] for u in un]
    md2_b = [_dot(md_b[u], md_b[u]).astype(BF16) for u in un]
    t1 = [_dot(md2_b[u], jnp.concatenate([x0[u].astype(BF16), md2_b[u]], axis=-1)) for u in un]
    x1 = [x0[u] + t1[u][:, :c] for u in un]
    md4_b = [t1[u][:, c:].astype(BF16) for u in un]
    xi = [x1[u] + _dot(md4_b[u], x1[u].astype(BF16)) for u in un]
    for mask in merge_masks:
        mo_b = [jnp.where(mask, mm[u], 0.0).astype(BF16) for u in un]
        xi_b = [xi[u].astype(BF16) for u in un]
        y_b = [_dot(xi_b[u], mo_b[u]).astype(BF16) for u in un]
        xi = [xi[u] - _dot(y_b[u], xi_b[u]) for u in un]
    xc_b = [(xi[u] - eye).astype(BF16) for u in un]

    rhs = [jnp.concatenate([kb[u] * eg[u], v[u] * bb[u]], axis=-1) for u in un]
    wu = [rhs[u] + _dot(xc_b[u], rhs[u].astype(BF16)) for u in un]
    wq_b = [jnp.concatenate([wu[u][:, :d], q[u] * eg[u]], axis=0).astype(BF16) for u in un]
    kg_b = [(k[u] * ek[u]).astype(BF16) for u in un]
    elast = [jnp.exp2(glast[u]) for u in un]

    state = [st_ref[h] for h in heads]
    for s in range(nsub):
        us = [s * GDN_HEADS + h for h in heads]
        state_b = [state[h].astype(BF16) for h in heads]
        ws = [_dot(wq_b[us[h]], state_b[h]) for h in heads]
        v_new_b = [(wu[us[h]][:, d:] - ws[h][:c]).astype(BF16) for h in heads]
        o = [ws[h][c:] + _dot(qk_b[us[h]], v_new_b[h]) for h in heads]
        state = [state[h] * elast[us[h]] + _dot_tn(kg_b[us[h]], v_new_b[h]) for h in heads]
        for h in heads:
            sl = slice(h * d, (h + 1) * d)
            rows = slice(s * c, (s + 1) * c)
            ms = jnp.mean(o[h] * o[h], axis=-1, keepdims=True)
            y = o[h] * lax.rsqrt(ms + RMS_EPS) * nw_ref[...]
            o_ref[rows, sl] = (y * _silu(z_ref[rows, sl].astype(F32))).astype(o_ref.dtype)
    for h in heads:
        st_ref[h] = state[h]

    buf_ref[0:HALO, :] = buf_ref[tt:tt + HALO, :]


def _gdn(proj, small, conv_w, a_log, dt_bias, gdn_norm_w, b, t, nsub=2):
    c = GDN_C
    tt = nsub * c
    nt = t // tt
    qkv_w = 3 * GDN_W
    pad = SMALL_W - GDN_HEADS
    al_row = jnp.pad(a_log.astype(F32), (0, pad)).reshape(1, SMALL_W)
    dtb_row = jnp.pad(dt_bias.astype(F32), (0, pad)).reshape(1, SMALL_W)
    tril = jnp.asarray(np.tril(np.ones((c, c), np.float32)), BF16)
    return pl.pallas_call(
        functools.partial(_gdn_kernel, nsub=nsub),
        grid=(b, nt),
        in_specs=[
            pl.BlockSpec((tt, qkv_w), lambda bi, ti: (bi * nt + ti, GDN_OFF // qkv_w)),
            pl.BlockSpec((tt, GDN_W), lambda bi, ti: (bi * nt + ti, (GDN_OFF + qkv_w) // GDN_W)),
            pl.BlockSpec((tt, SMALL_W), lambda bi, ti: (bi * nt + ti, 0)),
            pl.BlockSpec((GDN_CONV, qkv_w), lambda bi, ti: (0, 0)),
            pl.BlockSpec((1, SMALL_W), lambda bi, ti: (0, 0)),
            pl.BlockSpec((1, SMALL_W), lambda bi, ti: (0, 0)),
            pl.BlockSpec((1, HEAD_DIM), lambda bi, ti: (0, 0)),
            pl.BlockSpec((c, c), lambda bi, ti: (0, 0)),
        ],
        out_specs=pl.BlockSpec((tt, GDN_W), lambda bi, ti: (bi * nt + ti, 0)),
        out_shape=jax.ShapeDtypeStruct((b * t, GDN_W), BF16),
        scratch_shapes=[pltpu.VMEM((HALO + tt, qkv_w), F32),
                        pltpu.VMEM((GDN_HEADS, HEAD_DIM, HEAD_DIM), F32)],
        compiler_params=pltpu.CompilerParams(
            dimension_semantics=("parallel", "arbitrary"), vmem_limit_bytes=VMEM_LIMIT),
        name="gdn",
    )(proj, proj, small, conv_w, al_row, dtb_row, gdn_norm_w.reshape(1, HEAD_DIM), tril)


def _moba_key_features(t):
    pos = np.arange(t)
    blk = pos // MOBA_BLOCK
    off = pos % MOBA_BLOCK
    f = np.zeros((t, HEAD_DIM), np.float32)
    f[pos, blk] = 1.0
    f[:, SEL_LANES] = 1.0
    for p in range(3):
        f[:, SEL_LANES + 1 + p] = blk
        f[:, SEL_LANES + 4 + p] = off // 16
        f[:, SEL_LANES + 7 + p] = off % 16
    return jnp.asarray(f, BF16)


def _moba_query_features():
    hh = MOBA_HEADS
    slopes = np.exp2(-8.0 * np.arange(1, hh + 1, dtype=np.float64) / hh).astype(np.float32)
    s = jnp.asarray(slopes, F32)
    s1 = s.astype(BF16).astype(F32)
    s2 = (s - s1).astype(BF16).astype(F32)
    s3 = (s - s1 - s2).astype(BF16).astype(F32)
    f = jnp.zeros((hh, HEAD_DIM), F32)
    f = f.at[:, SEL_LANES].set(-s * MOBA_BLOCK)
    for p, sp in enumerate((s1, s2, s3)):
        f = f.at[:, SEL_LANES + 1 + p].set(sp * MOBA_BLOCK)
        f = f.at[:, SEL_LANES + 4 + p].set(sp * 16.0)
        f = f.at[:, SEL_LANES + 7 + p].set(sp)
    return f


def _moba_prep_kernel(k_ref, v_ref, nw_ref, pf_ref, ka_ref, va_ref, km_ref):
    ones = jnp.ones((k_ref.shape[0], HEAD_DIM), BF16)
    for h in range(MOBA_HEADS):
        sl = slice(h * HEAD_DIM, (h + 1) * HEAD_DIM)
        k = k_ref[:, sl].astype(F32)
        ms = jnp.mean(k * k, axis=-1, keepdims=True)
        kn = k * lax.rsqrt(ms + RMS_EPS) * nw_ref[...]
        ka_ref[0, h, :, 0:HEAD_DIM] = kn.astype(BF16)
        ka_ref[0, h, :, HEAD_DIM:2 * HEAD_DIM] = pf_ref[...]
        for j in range(k_ref.shape[0] // MOBA_BLOCK):
            rows = slice(j * MOBA_BLOCK, (j + 1) * MOBA_BLOCK)
            km_ref[0, j, :, sl] = jnp.mean(kn[rows], axis=0, keepdims=True)
        va_ref[0, h, :, 0:HEAD_DIM] = v_ref[:, sl]
        va_ref[0, h, :, HEAD_DIM:2 * HEAD_DIM] = ones


def _moba_prep(proj, k_norm_w, key_feat, b, t, bpt=4):
    blk = bpt * MOBA_BLOCK
    nb = t // blk
    kc = (MOBA_OFF + MOBA_W) // MOBA_W
    aug = pl.BlockSpec((1, MOBA_HEADS, blk, 2 * HEAD_DIM), lambda bi, ti: (bi, 0, ti, 0))
    aug_shape = jax.ShapeDtypeStruct((b, MOBA_HEADS, t, 2 * HEAD_DIM), BF16)
    return pl.pallas_call(
        _moba_prep_kernel,
        grid=(b, nb),
        in_specs=[
            pl.BlockSpec((blk, MOBA_W), lambda bi, ti: (bi * nb + ti, kc)),
            pl.BlockSpec((blk, MOBA_W), lambda bi, ti: (bi * nb + ti, kc + 1)),
            pl.BlockSpec((1, HEAD_DIM), lambda bi, ti: (0, 0)),
            pl.BlockSpec((blk, HEAD_DIM), lambda bi, ti: (ti, 0)),
        ],
        out_specs=[aug, aug, pl.BlockSpec((1, bpt, 1, MOBA_W), lambda bi, ti: (bi, ti, 0, 0))],
        out_shape=[aug_shape, aug_shape, jax.ShapeDtypeStruct((b, nb * bpt, 1, MOBA_W), F32)],
        compiler_params=pltpu.CompilerParams(
            dimension_semantics=("parallel", "parallel"), vmem_limit_bytes=VMEM_LIMIT),
        name="moba_prep",
    )(proj, proj, k_norm_w.reshape(1, HEAD_DIM), key_feat)


def _moba_qprep_kernel(q_ref, km_ref, qnw_ref, qf_ref, qa_ref):
    tq = q_ref.shape[0]
    nbp = km_ref.shape[1]
    t0 = pl.program_id(1) * tq
    lane = lax.broadcasted_iota(jnp.int32, (tq, HEAD_DIM), 1)
    rowi = lax.broadcasted_iota(jnp.int32, (tq, HEAD_DIM), 0)
    own = (t0 + rowi) // MOBA_BLOCK
    own_coef = jnp.where(lane == SEL_LANES, own.astype(F32), 1.0)
    blk_t = lax.broadcasted_iota(jnp.int32, (nbp, tq), 0)
    blk_f = blk_t.astype(F32)
    own_t = (t0 + lax.broadcasted_iota(jnp.int32, (nbp, tq), 1)) // MOBA_BLOCK
    past = blk_t < own_t
    heads = range(MOBA_HEADS)
    hsl = [slice(h * HEAD_DIM, (h + 1) * HEAD_DIM) for h in heads]

    def qnorm(h):
        q = q_ref[:, hsl[h]].astype(F32)
        ms = jnp.mean(q * q, axis=-1, keepdims=True)
        return q * lax.rsqrt(ms + RMS_EPS) * qnw_ref[...]

    qn = [qnorm(h) for h in heads]
    for h in heads:
        qa_ref[0, h, :, 0:HEAD_DIM] = (qn[h] * SCALE).astype(BF16)

    def gate_scores(h):
        km = km_ref[0, :, hsl[h]]
        q1 = qn[h].astype(BF16)
        q2 = (qn[h] - q1.astype(F32)).astype(BF16)
        k1 = km.astype(BF16)
        k2 = (km - k1.astype(F32)).astype(BF16)
        return jnp.where(past, _dot_nt(k1, q1) + _dot_nt(k2, q1) + _dot_nt(k1, q2), -jnp.inf)

    gate = [gate_scores(h) for h in heads]
    taken = [jnp.zeros((nbp, tq), F32) for _ in heads]
    no_block = float(nbp)
    for _ in range(MOBA_TOP_K):
        gm = [jnp.where(taken[h] > 0.0, -jnp.inf, gate[h]) for h in heads]
        mx = [jnp.max(gm[h], axis=0, keepdims=True) for h in heads]
        cand = [jnp.where(taken[h] > 0.0, no_block,
                          jnp.where(gm[h] == mx[h], blk_f, no_block)) for h in heads]
        idx = [jnp.min(cand[h], axis=0, keepdims=True) for h in heads]
        taken = [jnp.where(blk_f == idx[h], 1.0, taken[h]) for h in heads]
    zero_rows = jnp.zeros((HEAD_DIM - nbp, tq), F32)
    for h in heads:
        sel_t = jnp.where(blk_t == own_t, 0.0,
                          jnp.where(past, jnp.where(taken[h] > 0.0, 0.0, NEG_BIG), NEG_BIG))
        sel_bias = jnp.concatenate([sel_t, zero_rows], axis=0).T
        extra = jnp.where(lane < SEL_LANES, sel_bias, qf_ref[h:h + 1, :] * own_coef)
        qa_ref[0, h, :, HEAD_DIM:2 * HEAD_DIM] = extra.astype(BF16)


def _moba_qprep(proj, k_mean, q_norm_w, q_feat, b, t, tq=512):
    nq = t // tq
    nbp = k_mean.shape[1]
    return pl.pallas_call(
        _moba_qprep_kernel,
        grid=(b, nq),
        in_specs=[
            pl.BlockSpec((tq, MOBA_W), lambda bi, ti: (bi * nq + ti, MOBA_OFF // MOBA_W)),
            pl.BlockSpec((1, nbp, MOBA_W), lambda bi, ti: (bi, 0, 0)),
            pl.BlockSpec((1, HEAD_DIM), lambda bi, ti: (0, 0)),
            pl.BlockSpec((MOBA_HEADS, HEAD_DIM), lambda bi, ti: (0, 0)),
        ],
        out_specs=pl.BlockSpec((1, MOBA_HEADS, tq, 2 * HEAD_DIM), lambda bi, ti: (bi, 0, ti, 0)),
        out_shape=jax.ShapeDtypeStruct((b, MOBA_HEADS, t, 2 * HEAD_DIM), BF16),
        compiler_params=pltpu.CompilerParams(
            dimension_semantics=("parallel", "parallel"), vmem_limit_bytes=VMEM_LIMIT),
        name="moba_qprep",
    )(proj, k_mean, q_norm_w.reshape(1, HEAD_DIM), q_feat)


def _moba_attn_kernel(qa_ref, ka_ref, va_ref, z_ref, o_ref, s_ref, m_ref, acc_ref, *, group, hp, tq):
    blk = MOBA_BLOCK
    d = HEAD_DIM
    gw = group * blk
    qi = pl.program_id(2)
    n_full = (qi * tq) // gw
    heads = range(hp)

    m_ref[...] = jnp.full(m_ref.shape, NEG_BIG, F32)
    acc_ref[...] = jnp.zeros(acc_ref.shape, F32)

    def store_scores(slot, g):
        off = pl.multiple_of(g * gw, gw)
        for h in heads:
            s_ref[slot, h] = _dot_nt(qa_ref[0, h], ka_ref[0, h, pl.ds(off, gw), :])

    def update(s, g):
        off = pl.multiple_of(g * gw, gw)
        m_prev = [m_ref[h] for h in heads]
        m_new = [jnp.maximum(m_prev[h], jnp.max(s[h], axis=-1, keepdims=True)) for h in heads]
        alpha = [jnp.exp(m_prev[h] - m_new[h]) for h in heads]
        p = [jnp.exp(s[h] - jnp.concatenate([m_new[h]] * (gw // d), axis=-1)).astype(BF16)
             for h in heads]
        for h in heads:
            m_ref[h] = m_new[h]
        pv = [_dot(p[h], va_ref[0, h, pl.ds(off, gw), :]) for h in heads]
        for h in heads:
            acc_ref[h] = jnp.concatenate([alpha[h], alpha[h]], axis=-1) * acc_ref[h] + pv[h]

    def step(slot, g):
        s = [s_ref[slot, h] for h in heads]
        update(s, g)
        store_scores(1 - slot, g + 1)

    store_scores(0, 0)

    def body(i, carry):
        step(0, 2 * i)
        step(1, 2 * i + 1)
        return carry

    lax.fori_loop(0, n_full // 2, body, 0)

    @pl.when(n_full % 2 == 1)
    def _():
        step(0, n_full - 1)

    qpos = qi * tq + lax.broadcasted_iota(jnp.int32, (tq, gw), 0)
    kpos = n_full * gw + lax.broadcasted_iota(jnp.int32, (tq, gw), 1)
    visible = kpos <= qpos
    slot = n_full % 2
    update([jnp.where(visible, s_ref[slot, h], NEG_BIG) for h in heads], n_full)

    for h in heads:
        sl = slice(h * d, (h + 1) * d)
        o = acc_ref[h, :, 0:d] / acc_ref[h, :, d:2 * d]
        o_ref[:, sl] = (o * _silu(z_ref[:, sl].astype(F32))).astype(o_ref.dtype)


def _moba_attn(proj, q_aug, k_aug, v_aug, b, t, group=4, hp=2, tq=512):
    blk = MOBA_BLOCK
    nb = t // blk
    nq = t // tq
    w = hp * HEAD_DIM
    zc = (MOBA_OFF + 3 * MOBA_W) // w
    assert nb % group == 0 and MOBA_HEADS % hp == 0 and MOBA_W % w == 0 and MOBA_OFF % w == 0
    assert tq % blk == 0 and (group * blk) % tq == 0
    resident = pl.BlockSpec((1, hp, t, 2 * HEAD_DIM), lambda bi, hi, qi: (bi, hi, 0, 0))
    return pl.pallas_call(
        functools.partial(_moba_attn_kernel, group=group, hp=hp, tq=tq),
        grid=(b, MOBA_HEADS // hp, nq),
        in_specs=[
            pl.BlockSpec((1, hp, tq, 2 * HEAD_DIM), lambda bi, hi, qi: (bi, hi, qi, 0)),
            resident,
            resident,
            pl.BlockSpec((tq, w), lambda bi, hi, qi: (bi * nq + qi, zc + hi)),
        ],
        out_specs=pl.BlockSpec((tq, w), lambda bi, hi, qi: (bi * nq + qi, hi)),
        out_shape=jax.ShapeDtypeStruct((b * t, MOBA_W), BF16),
        scratch_shapes=[pltpu.VMEM((2, hp, tq, group * blk), F32),
                        pltpu.VMEM((hp, tq, HEAD_DIM), F32),
                        pltpu.VMEM((hp, tq, 2 * HEAD_DIM), F32)],
        compiler_params=pltpu.CompilerParams(
            dimension_semantics=("parallel", "parallel", "arbitrary"), vmem_limit_bytes=VMEM_LIMIT),
        name="moba_attn",
    )(q_aug, k_aug, v_aug, proj)


def _layer(x2d, h2d, b, t, layer, w_a, w_b, conv_w, a_log, dt_bias, ret_norm_w, gdn_norm_w,
           q_norm_w, k_norm_w, w_out, next_norm_w, key_feat, q_feat):
    nb = t // MOBA_BLOCK
    proj, small = _in_proj(h2d, w_a, w_b, layer)
    y_ret = _retention(proj, ret_norm_w, b, t)
    y_gdn = _gdn(proj, small, conv_w, a_log, dt_bias, gdn_norm_w, b, t)
    k_aug, v_aug, k_mean = _moba_prep(proj, k_norm_w, key_feat, b, t)
    k_mean = jnp.pad(k_mean.reshape(b, nb, MOBA_W), ((0, 0), (0, -nb % 8), (0, 0)))
    q_aug = _moba_qprep(proj, k_mean, q_norm_w, q_feat, b, t)
    y_moba = _moba_attn(proj, q_aug, k_aug, v_aug, b, t)
    return _out_proj(x2d, y_ret, y_gdn, y_moba, w_out, layer, next_norm_w)


def kernel(x, norm_w, w_in, conv_w, a_log, dt_bias, ret_norm_w, gdn_norm_w, q_norm_w, k_norm_w, w_out):
    b, t, d = x.shape
    depth = w_in.shape[0]
    assert t % MOBA_BLOCK == 0 and t // MOBA_BLOCK <= SEL_LANES
    moba_lo = 4 * RET_W + 4 * GDN_W + 2 * GDN_HEADS
    w_a = jnp.swapaxes(w_in, 1, 2).astype(BF16)
    w_b = w_a[:, moba_lo:, :]
    w_out_b = w_out.astype(BF16)
    key_feat = _moba_key_features(t)
    q_feat = _moba_query_features()
    x2d = x.reshape(b * t, d)
    h2d = _norm(x2d, norm_w[0])
    for layer in range(depth):
        next_norm_w = norm_w[layer + 1] if layer + 1 < depth else None
        x2d, h2d = _layer(x2d, h2d, b, t, layer, w_a, w_b, conv_w[layer], a_log[layer],
                          dt_bias[layer], ret_norm_w[layer], gdn_norm_w[layer], q_norm_w[layer],
                          k_norm_w[layer], w_out_b, next_norm_w, key_feat, q_feat)
    return x2d.reshape(b, t, d)
```
